```python
import math
import jax, jax.numpy as jnp
from jax import lax
import numpy as np

D_MODEL = 2048
BATCH = 4
SEQ = 4096
DEPTH = 4

DIFF_HEADS = 4
DIFF_DK = 64
DIFF_DV = 2 * DIFF_DK
MLA_HEADS = 4
MLA_Q_RANK = 512
MLA_KV_RANK = 256
MLA_NOPE = 128
MLA_ROPE = 64
MLA_DV = 128
ROPE_THETA = 10000.0
SWA_Q_HEADS = 8
SWA_KV_HEADS = 2
SWA_DH = 64
SWA_WINDOW = 128
SWA_BLOCK = 128
MOBA_HEADS = 4
MOBA_DH = 128
MOBA_BLOCK = 256
MOBA_TOPK = 3
MOBA_QCHUNK = 64
Q_BLOCK = 128
REL_BUCKETS = 32
REL_MAX_DIST = 128
BIAS_A0, BIAS_A1 = 0, DIFF_HEADS
BIAS_C0, BIAS_C1 = DIFF_HEADS, DIFF_HEADS + SWA_Q_HEADS
BIAS_D0, BIAS_D1 = DIFF_HEADS + SWA_Q_HEADS, DIFF_HEADS + SWA_Q_HEADS + MOBA_HEADS
BIAS_HEADS = BIAS_D1
N_BRANCH = 4
BRANCH_W = 512
FFN_DIM = 5632
CONV_W = 3
EPS = 1e-6
NEG_INF = -1e30

IN_SPLITS = (
    DIFF_HEADS * 2 * DIFF_DK, DIFF_HEADS * 2 * DIFF_DK, DIFF_HEADS * DIFF_DV,
    MLA_Q_RANK, MLA_KV_RANK, MLA_ROPE,
    SWA_Q_HEADS * SWA_DH, SWA_KV_HEADS * SWA_DH, SWA_KV_HEADS * SWA_DH,
    MOBA_HEADS * MOBA_DH, MOBA_HEADS * MOBA_DH, MOBA_HEADS * MOBA_DH,
    N_BRANCH * D_MODEL,
)
IN_COLS = sum(IN_SPLITS)

kernel_name = 'hybrid_gated_diff_mla_swa_moba_convglu'

F32 = jnp.float32


def rms_norm(x, g):
    xf = x.astype(F32)
    y = xf * lax.rsqrt(jnp.mean(xf * xf, axis=-1, keepdims=True) + EPS)
    return (y * g.astype(F32)).astype(x.dtype)


def t5_bucket(rel):
    n = jnp.maximum(rel, 0)
    max_exact = REL_BUCKETS // 2
    nf = jnp.maximum(n, 1).astype(F32)
    large = max_exact + (jnp.log(nf / max_exact) / math.log(REL_MAX_DIST / max_exact)
                         * (REL_BUCKETS - max_exact)).astype(jnp.int32)
    return jnp.where(n < max_exact, n, jnp.minimum(large, REL_BUCKETS - 1))


def rope(x, positions):
    dr = x.shape[-1]
    inv = ROPE_THETA ** (-jnp.arange(0, dr, 2, dtype=F32) / dr)
    ang = positions.astype(F32)[:, None] * inv[None, :]
    cos, sin = jnp.cos(ang), jnp.sin(ang)
    xf = x.astype(F32)
    x1, x2 = xf[..., : dr // 2], xf[..., dr // 2:]
    return jnp.concatenate([x1 * cos - x2 * sin, x1 * sin + x2 * cos], axis=-1).astype(x.dtype)


def dense_causal_attention(q, k, v, scale, positions, bias_tab):
    B, H, S, _ = q.shape
    kidx = jnp.arange(S)

    def block(i):
        start = i * Q_BLOCK
        qb = lax.dynamic_slice_in_dim(q, start, Q_BLOCK, axis=2)
        s = jnp.einsum('bhqd,bhkd->bhqk', qb, k, preferred_element_type=F32) * scale
        qidx = start + jnp.arange(Q_BLOCK)
        if bias_tab is not None:
            qpos = lax.dynamic_slice_in_dim(positions, start, Q_BLOCK)
            bkt = t5_bucket(qpos[:, None] - positions[None, :])
            s = s + jnp.moveaxis(bias_tab[bkt].astype(F32), -1, 0)[None]
        s = jnp.where(kidx[None, :] <= qidx[:, None], s, NEG_INF)
        p = jax.nn.softmax(s, axis=-1)
        return jnp.einsum('bhqk,bhkd->bhqd', p.astype(v.dtype), v)

    out = lax.map(block, jnp.arange(S // Q_BLOCK))
    return jnp.moveaxis(out, 0, 2).reshape(B, H, S, v.shape[-1])


def diff_attention(zq, zk, zv, lam_vecs, subln_g, lam_init, positions, bias_tab):
    B, S, _ = zq.shape
    H = DIFF_HEADS
    q = zq.reshape(B, S, H * 2, DIFF_DK).transpose(0, 2, 1, 3)
    k = zk.reshape(B, S, H * 2, DIFF_DK).transpose(0, 2, 1, 3)
    v = zv.reshape(B, S, H, DIFF_DV).transpose(0, 2, 1, 3)
    v = jnp.repeat(v, 2, axis=1)
    o = dense_causal_attention(q, k, v, DIFF_DK ** -0.5, positions, jnp.repeat(bias_tab, 2, axis=1))
    o = o.reshape(B, H, 2, S, DIFF_DV).astype(F32)
    lv = lam_vecs.astype(F32)
    lam = jnp.exp(jnp.sum(lv[0] * lv[1])) - jnp.exp(jnp.sum(lv[2] * lv[3])) + lam_init
    o = o[:, :, 0] - lam * o[:, :, 1]
    o = rms_norm(o, subln_g) * (1.0 - lam_init)
    return o.transpose(0, 2, 1, 3).reshape(B, S, H * DIFF_DV).astype(zq.dtype)


def mla_attention(zcq, zckv, zkr, gq, w_uq, gkv, w_ukv, positions):
    B, S, _ = zcq.shape
    H = MLA_HEADS
    cq = rms_norm(zcq, gq)
    q = (cq @ w_uq).reshape(B, S, H, MLA_NOPE + MLA_ROPE).transpose(0, 2, 1, 3)
    q = jnp.concatenate([q[..., :MLA_NOPE], rope(q[..., MLA_NOPE:], positions)], axis=-1)
    ckv = rms_norm(zckv, gkv)
    kv = (ckv @ w_ukv).reshape(B, S, H, MLA_NOPE + MLA_DV).transpose(0, 2, 1, 3)
    k_rope = jnp.broadcast_to(rope(zkr, positions)[:, None], (B, H, S, MLA_ROPE))
    k = jnp.concatenate([kv[..., :MLA_NOPE], k_rope], axis=-1)
    v = kv[..., MLA_NOPE:]
    o = dense_causal_attention(q, k, v, (MLA_NOPE + MLA_ROPE) ** -0.5, positions, None)
    return o.transpose(0, 2, 1, 3).reshape(B, S, H * MLA_DV)


def swa_sink_attention(zq, zk, zv, sinks, positions, bias_tab):
    B, S, _ = zq.shape
    KVH, G, L, d = SWA_KV_HEADS, SWA_Q_HEADS // SWA_KV_HEADS, SWA_BLOCK, SWA_DH
    nb = S // L
    q = zq.reshape(B, nb, L, KVH, G, d)

    def band(t):
        tb = jnp.pad(t, ((0, 0), (L, 0), (0, 0), (0, 0))).reshape(B, nb + 1, L, KVH, d)
        return jnp.concatenate([tb[:, :-1], tb[:, 1:]], axis=2)

    k = band(zk.reshape(B, S, KVH, d))
    v = band(zv.reshape(B, S, KVH, d))
    s = jnp.einsum('bnqhgd,bnkhd->bhgnqk', q, k, preferred_element_type=F32) * d ** -0.5
    qidx = jnp.arange(nb)[:, None] * L + jnp.arange(L)[None, :]
    kidx = jnp.arange(nb)[:, None] * L - L + jnp.arange(2 * L)[None, :]
    dist = qidx[:, :, None] - kidx[:, None, :]
    valid = (dist >= 0) & (dist < SWA_WINDOW) & (kidx[:, None, :] >= 0)
    pos_p = jnp.pad(positions, (L, 0))
    bkt = t5_bucket(positions[qidx][:, :, None] - pos_p[kidx + L][:, None, :])
    bias = bias_tab[bkt].astype(F32).reshape(nb, L, 2 * L, KVH, G).transpose(3, 4, 0, 1, 2)
    s = jnp.where(valid, s + bias, NEG_INF)
    sink = sinks.astype(F32).reshape(KVH, G)[:, :, None, None, None]
    m = jnp.maximum(jnp.max(s, axis=-1, keepdims=True), sink)
    p = jnp.exp(s - m)
    p = p / (jnp.sum(p, axis=-1, keepdims=True) + jnp.exp(sink - m))
    o = jnp.einsum('bhgnqk,bnkhd->bnqhgd', p.astype(v.dtype), v)
    return o.reshape(B, S, SWA_Q_HEADS * d)


def moba_attention(zq, zk, zv, positions, bias_tab):
    B, S, _ = zq.shape
    H, d, L, QC = MOBA_HEADS, MOBA_DH, MOBA_BLOCK, MOBA_QCHUNK
    nblk = -(-S // L)
    Sp = nblk * L
    n_sel = min(MOBA_TOPK, nblk - 1)

    def to_heads(t):
        return t.reshape(B, S, H, d).transpose(0, 2, 1, 3)

    pad = ((0, 0), (0, 0), (0, Sp - S), (0, 0))
    q = to_heads(zq)
    k = jnp.pad(to_heads(zk), pad)
    v = jnp.pad(to_heads(zv), pad)
    pos_p = jnp.pad(positions, (0, Sp - S))
    k_blocks = k.reshape(B, H, nblk, L, d)
    v_blocks = v.reshape(B, H, nblk, L, d)
    k_mean = jnp.mean(k_blocks.astype(F32), axis=3)
    bias_T = bias_tab.astype(F32).T
    b_ix = jnp.arange(B)[:, None, None, None]
    h_ix = jnp.arange(H)[None, :, None, None]
    scale = d ** -0.5

    def chunk(c):
        start = c * QC
        qc = lax.dynamic_slice_in_dim(q, start, QC, axis=2)
        qidx = start + jnp.arange(QC)
        qpos = lax.dynamic_slice_in_dim(positions, start, QC)
        blk = start // L
        ks = lax.dynamic_slice_in_dim(k, blk * L, L, axis=2)
        vs = lax.dynamic_slice_in_dim(v, blk * L, L, axis=2)
        kidx = blk * L + jnp.arange(L)
        s_own = jnp.einsum('bhqd,bhkd->bhqk', qc, ks, preferred_element_type=F32) * scale
        s_own = s_own + bias_T[:, t5_bucket(qpos[:, None] - pos_p[kidx][None, :])][None]
        s_own = jnp.where(kidx[None, :] <= qidx[:, None], s_own, NEG_INF)
        if n_sel == 0:
            p = jax.nn.softmax(s_own, axis=-1)
            return jnp.einsum('bhqk,bhkd->bhqd', p.astype(vs.dtype), vs)
        gate = jnp.einsum('bhqd,bhnd->bhqn', qc.astype(F32), k_mean)
        gate = jnp.where(jnp.arange(nblk) < blk, gate, NEG_INF)
        _, sel = lax.top_k(gate, n_sel)
        sel_ok = jnp.repeat(jnp.arange(n_sel) < blk, L)
        k_sel = k_blocks[b_ix, h_ix, sel].reshape(B, H, QC, n_sel * L, d)
        v_sel = v_blocks[b_ix, h_ix, sel].reshape(B, H, QC, n_sel * L, d)
        s_sel = jnp.einsum('bhqd,bhqkd->bhqk', qc, k_sel, preferred_element_type=F32) * scale
        kidx_sel = (sel[..., None] * L + jnp.arange(L)).reshape(B, H, QC, n_sel * L)
        bkt = t5_bucket(qpos[:, None] - pos_p[kidx_sel])
        s_sel = jnp.where(sel_ok, s_sel + bias_T[jnp.arange(H)[:, None, None], bkt], NEG_INF)
        p = jax.nn.softmax(jnp.concatenate([s_sel, s_own], axis=-1), axis=-1)
        p_sel = p[..., : n_sel * L].astype(v_sel.dtype)
        p_own = p[..., n_sel * L:].astype(vs.dtype)
        return (jnp.einsum('bhqk,bhqkd->bhqd', p_sel, v_sel)
                + jnp.einsum('bhqk,bhkd->bhqd', p_own, vs))

    out = lax.map(chunk, jnp.arange(S // QC))
    out = jnp.moveaxis(out, 0, 2).reshape(B, H, S, d)
    return out.transpose(0, 2, 1, 3).reshape(B, S, H * d)


def conv_glu(h, w_up, conv_w, conv_b, w_down):
    S = h.shape[1]
    u = h @ w_up
    a, val = u[..., :FFN_DIM], u[..., FFN_DIM:]
    ap = jnp.pad(a, ((0, 0), (CONV_W - 1, 0), (0, 0)))
    a = conv_b + conv_w[0] * ap[:, 0:S]
    for j in range(1, CONV_W):
        a = a + conv_w[j] * ap[:, j:j + S]
    return (jax.nn.gelu(a, approximate=False) * val) @ w_down


def setup_inputs(seed: int = 0) -> dict:
    key = jax.random.key(seed)
    ks = jax.random.split(key, 20)

    def nrm(k, shape, scale):
        return jax.random.normal(k, shape, F32) * scale

    def gain(k, shape):
        return 1.0 + 0.02 * jax.random.normal(k, shape, F32)

    return {
        'x': nrm(ks[0], (BATCH, SEQ, D_MODEL), 1.0),
        'positions': jnp.arange(SEQ, dtype=jnp.int32),
        'rel_bias': nrm(ks[1], (REL_BUCKETS, BIAS_HEADS), 0.5),
        'norm1_g': gain(ks[2], (DEPTH, D_MODEL)),
        'w_in': nrm(ks[3], (DEPTH, D_MODEL, IN_COLS), D_MODEL ** -0.5),
        'diff_lambda': nrm(ks[4], (DEPTH, 4, DIFF_DK), 0.1),
        'diff_subln_g': gain(ks[5], (DEPTH, DIFF_DV)),
        'mla_q_norm_g': gain(ks[6], (DEPTH, MLA_Q_RANK)),
        'mla_w_uq': nrm(ks[7], (DEPTH, MLA_Q_RANK, MLA_HEADS * (MLA_NOPE + MLA_ROPE)), MLA_Q_RANK ** -0.5),
        'mla_kv_norm_g': gain(ks[8], (DEPTH, MLA_KV_RANK)),
        'mla_w_ukv': nrm(ks[9], (DEPTH, MLA_KV_RANK, MLA_HEADS * (MLA_NOPE + MLA_DV)), MLA_KV_RANK ** -0.5),
        'swa_sinks': nrm(ks[10], (DEPTH, SWA_Q_HEADS), 0.5),
        'w_branch': nrm(ks[11], (DEPTH, N_BRANCH, BRANCH_W, D_MODEL), BRANCH_W ** -0.5),
        'w_out': nrm(ks[12], (DEPTH, D_MODEL, D_MODEL), D_MODEL ** -0.5),
        'norm2_g': gain(ks[13], (DEPTH, D_MODEL)),
        'ffn_w_up': nrm(ks[14], (DEPTH, D_MODEL, 2 * FFN_DIM), D_MODEL ** -0.5),
        'ffn_conv_w': nrm(ks[15], (DEPTH, CONV_W, FFN_DIM), CONV_W ** -0.5),
        'ffn_conv_b': nrm(ks[16], (DEPTH, FFN_DIM), 0.01),
        'ffn_w_down': nrm(ks[17], (DEPTH, FFN_DIM, D_MODEL), FFN_DIM ** -0.5),
        'final_norm_g': gain(ks[18], (D_MODEL,)),
    }


def reference(x, positions, rel_bias, norm1_g, w_in, diff_lambda, diff_subln_g,
              mla_q_norm_g, mla_w_uq, mla_kv_norm_g, mla_w_ukv, swa_sinks, w_branch,
              w_out, norm2_g, ffn_w_up, ffn_conv_w, ffn_conv_b, ffn_w_down, final_norm_g):
    B, S, D = x.shape
    split_points = [int(i) for i in np.cumsum(IN_SPLITS)[:-1]]
    bias_a = rel_bias[:, BIAS_A0:BIAS_A1]
    bias_c = rel_bias[:, BIAS_C0:BIAS_C1]
    bias_d = rel_bias[:, BIAS_D0:BIAS_D1]
    for l in range(DEPTH):
        lam_init = 0.8 - 0.6 * math.exp(-0.3 * l)
        h = rms_norm(x, norm1_g[l])
        z = h @ w_in[l]
        (aq, ak, av, cq, ckv, kr, sq, sk, sv, mq, mk, mv, zg) = jnp.split(z, split_points, axis=-1)
        branches = (
            diff_attention(aq, ak, av, diff_lambda[l], diff_subln_g[l], lam_init, positions, bias_a),
            mla_attention(cq, ckv, kr, mla_q_norm_g[l], mla_w_uq[l], mla_kv_norm_g[l], mla_w_ukv[l], positions),
            swa_sink_attention(sq, sk, sv, swa_sinks[l], positions, bias_c),
            moba_attention(mq, mk, mv, positions, bias_d),
        )
        gates = jax.nn.sigmoid(zg.astype(F32)).astype(x.dtype).reshape(B, S, N_BRANCH, D)
        merged = gates[:, :, 0] * (branches[0] @ w_branch[l, 0])
        for i in range(1, N_BRANCH):
            merged = merged + gates[:, :, i] * (branches[i] @ w_branch[l, i])
        x = x + merged @ w_out[l]
        x = x + conv_glu(rms_norm(x, norm2_g[l]), ffn_w_up[l], ffn_conv_w[l], ffn_conv_b[l], ffn_w_down[l])
    return rms_norm(x, final_norm_g)
```

```python
import functools
import math

import numpy as np
import jax
import jax.numpy as jnp
from jax import lax
from jax.experimental import pallas as pl
from jax.experimental.pallas import tpu as pltpu

F32 = jnp.float32
BF16 = jnp.bfloat16

DIFF_HEADS = 4
DIFF_DK = 64
DIFF_DV = 128
MLA_HEADS = 4
MLA_Q_RANK = 512
MLA_KV_RANK = 256
MLA_NOPE = 128
MLA_ROPE = 64
MLA_DV = 128
ROPE_THETA = 10000.0
SWA_Q_HEADS = 8
SWA_KV_HEADS = 2
SWA_GROUP = SWA_Q_HEADS // SWA_KV_HEADS
SWA_DH = 64
SWA_WINDOW = 128
MOBA_HEADS = 4
MOBA_DH = 128
MOBA_BLOCK = 256
MOBA_TOPK = 3
REL_BUCKETS = 32
REL_MAX_DIST = 128
BIAS_HEADS = DIFF_HEADS + SWA_Q_HEADS + MOBA_HEADS
BIAS_C0 = DIFF_HEADS
BIAS_D0 = DIFF_HEADS + SWA_Q_HEADS
N_BRANCH = 4
BRANCH_W = 512
CONV_W = 3
EPS = 1e-6
NEG_INF = -1e30

LANES = 128
ATT_TQ = 256
VMEM_LIMIT = 56 * 1024 * 1024

QKV_W = 5120
OFF_AQ, OFF_AK, OFF_AV = 0, 512, 1024
OFF_MQ, OFF_MK, OFF_MV = 1536, 2048, 2560
OFF_CQ, OFF_CKV, OFF_KR = 3072, 3584, 3840
OFF_SQ, OFF_SK, OFF_SV = 3968, 4480, 4608
QKV_USED = 4736

_NT = (((1,), (1,)), ((), ()))


def _t5_thresholds():
    n = np.arange(0, 2 * REL_MAX_DIST)
    max_exact = REL_BUCKETS // 2
    nf = np.maximum(n, 1).astype(np.float32)
    large = max_exact + (np.log(nf / np.float32(max_exact)) / np.float32(math.log(REL_MAX_DIST / max_exact))
                         * np.float32(REL_BUCKETS - max_exact)).astype(np.int32)
    bkt = np.where(n < max_exact, n, np.minimum(large, REL_BUCKETS - 1))
    return tuple(int(np.argmax(bkt >= b)) for b in range(1, REL_BUCKETS))


T5_THR = _t5_thresholds()


def _cparams(sem):
    return pltpu.CompilerParams(dimension_semantics=sem, vmem_limit_bytes=VMEM_LIMIT)


def _rmsnorm_rows(x, g):
    ms = jnp.mean(x * x, axis=-1, keepdims=True)
    return x * lax.rsqrt(ms + EPS) * g


def _in_proj_kernel(x_ref, g_ref, w_ref, z_ref, h_ref, *, rows):
    @pl.when(pl.program_id(1) == 0)
    def _():
        def body(c, carry):
            r = pl.multiple_of(c * rows, rows)
            h_ref[pl.ds(r, rows), :] = _rmsnorm_rows(x_ref[pl.ds(r, rows), :], g_ref[...]).astype(BF16)
            return carry
        lax.fori_loop(0, x_ref.shape[0] // rows, body, 0)

    z_ref[...] = jnp.dot(h_ref[...], w_ref[...], preferred_element_type=F32).astype(z_ref.dtype)


def _in_proj(x, g, w):
    t, d = x.shape
    nz = w.shape[1]
    tm = min(1024, t)
    tn = 1024
    rows = min(256, tm)
    return pl.pallas_call(
        functools.partial(_in_proj_kernel, rows=rows),
        grid=(t // tm, nz // tn),
        in_specs=[pl.BlockSpec((tm, d), lambda i, j: (i, 0)),
                  pl.BlockSpec((1, d), lambda i, j: (0, 0)),
                  pl.BlockSpec((d, tn), lambda i, j: (0, j))],
        out_specs=pl.BlockSpec((tm, tn), lambda i, j: (i, j)),
        out_shape=jax.ShapeDtypeStruct((t, nz), BF16),
        scratch_shapes=[pltpu.VMEM((tm, d), BF16)],
        compiler_params=_cparams(("parallel", "arbitrary")),
        name="in_proj",
    )(x, g, w)


def _bias_table_kernel(tab_ref, tz_ref, sw_ref, *, tq):
    h = pl.program_id(0)

    def lookup(n):
        val = jnp.full(n.shape, tab_ref[0, h], F32)
        for b in range(1, REL_BUCKETS):
            val = jnp.where(n >= T5_THR[b - 1], tab_ref[b, h], val)
        return val

    far = tab_ref[REL_BUCKETS - 1, h]
    r = lax.broadcasted_iota(jnp.int32, (tq, tq), 0)
    c = lax.broadcasted_iota(jnp.int32, (tq, tq), 1)
    rel = r - c
    tz_ref[0, 0] = jnp.where(rel >= 0, lookup(jnp.maximum(rel, 0)) - far, NEG_INF)
    tz_ref[0, 1] = lookup(rel + tq) - far
    r = lax.broadcasted_iota(jnp.int32, (SWA_WINDOW, 2 * SWA_WINDOW), 0)
    c = lax.broadcasted_iota(jnp.int32, (SWA_WINDOW, 2 * SWA_WINDOW), 1)
    rel = r - c + SWA_WINDOW
    valid = (rel >= 0) & (rel < SWA_WINDOW)
    sw_ref[0] = jnp.where(valid, lookup(jnp.maximum(rel, 0)), NEG_INF)


def _bias_tables(rel_bias):
    tq = ATT_TQ
    return pl.pallas_call(
        functools.partial(_bias_table_kernel, tq=tq),
        grid=(BIAS_HEADS,),
        in_specs=[pl.BlockSpec(memory_space=pltpu.SMEM)],
        out_specs=[pl.BlockSpec((1, 2, tq, tq), lambda h: (h, 0, 0, 0)),
                   pl.BlockSpec((1, SWA_WINDOW, 2 * SWA_WINDOW), lambda h: (h, 0, 0))],
        out_shape=[jax.ShapeDtypeStruct((BIAS_HEADS, 2, tq, tq), F32),
                   jax.ShapeDtypeStruct((BIAS_HEADS, SWA_WINDOW, 2 * SWA_WINDOW), F32)],
        compiler_params=_cparams(("arbitrary",)),
        name="bias_tables",
    )(rel_bias)


def _attn_first(s, v, m_ref, l_ref, acc_ref):
    m = jnp.max(s, axis=-1, keepdims=True)
    p = jnp.exp(s - m)
    m_ref[...] = m
    l_ref[...] = jnp.sum(p, axis=-1, keepdims=True)
    acc_ref[...] = jnp.dot(p.astype(v.dtype), v, preferred_element_type=F32)


def _attn_next(s, v, m_ref, l_ref, acc_ref):
    m_old = m_ref[...]
    m_new = jnp.maximum(m_old, jnp.max(s, axis=-1, keepdims=True))
    alpha = jnp.exp(m_old - m_new)
    p = jnp.exp(s - m_new)
    m_ref[...] = m_new
    l_ref[...] = alpha * l_ref[...] + jnp.sum(p, axis=-1, keepdims=True)
    acc_ref[...] = alpha * acc_ref[...] + jnp.dot(p.astype(v.dtype), v, preferred_element_type=F32)


def _kv_tile(k_ref, v_ref, j, tq):
    start = pl.multiple_of(j * tq, tq)
    return k_ref[pl.ds(start, tq), :], v_ref[pl.ds(start, tq), :]


def _diff_attn_kernel(q_ref, k_ref, v_ref, tz_ref, lam_ref, g_ref, o_ref,
                      m1, l1, a1, m2, l2, a2, *, tq, lam_init):
    qi = pl.program_id(2)
    qs = q_ref[...] * (DIFF_DK ** -0.5)
    lane = lax.broadcasted_iota(jnp.int32, qs.shape, 1)
    zero = jnp.zeros_like(qs)
    qa = jnp.where(lane < DIFF_DK, qs, zero)
    qb = jnp.where(lane >= DIFF_DK, qs, zero)

    def scores(kt):
        return (lax.dot_general(qa, kt, _NT, preferred_element_type=F32),
                lax.dot_general(qb, kt, _NT, preferred_element_type=F32))

    kt, vt = _kv_tile(k_ref, v_ref, qi, tq)
    s1, s2 = scores(kt)
    b0 = tz_ref[0, 0]
    _attn_first(s1 + b0, vt, m1, l1, a1)
    _attn_first(s2 + b0, vt, m2, l2, a2)

    @pl.when(qi >= 1)
    def _():
        kt, vt = _kv_tile(k_ref, v_ref, qi - 1, tq)
        s1, s2 = scores(kt)
        b1 = tz_ref[0, 1]
        _attn_next(s1 + b1, vt, m1, l1, a1)
        _attn_next(s2 + b1, vt, m2, l2, a2)

    def body(j, carry):
        kt, vt = _kv_tile(k_ref, v_ref, j, tq)
        s1, s2 = scores(kt)
        _attn_next(s1, vt, m1, l1, a1)
        _attn_next(s2, vt, m2, l2, a2)
        return carry

    lax.fori_loop(0, qi - 1, body, 0)

    lv = lam_ref[...]
    e1 = jnp.exp(jnp.sum(lv[0:1] * lv[1:2], axis=-1, keepdims=True))
    e2 = jnp.exp(jnp.sum(lv[2:3] * lv[3:4], axis=-1, keepdims=True))
    lam = e1 - e2 + lam_init
    o = a1[...] / l1[...] - lam * (a2[...] / l2[...])
    o = o * lax.rsqrt(jnp.mean(o * o, axis=-1, keepdims=True) + EPS) * g_ref[...] * (1.0 - lam_init)
    o_ref[...] = o.astype(o_ref.dtype)


def _diff_attention(z, tz, lam_vecs, subln_g, lam_init, *, batch, seq, qkv_blk):
    tq = ATT_TQ
    nq = seq // tq
    qb, kb, vb = (qkv_blk + off // LANES for off in (OFF_AQ, OFF_AK, OFF_AV))
    stat = pltpu.VMEM((tq, 1), F32)
    accs = pltpu.VMEM((tq, DIFF_DV), F32)
    return pl.pallas_call(
        functools.partial(_diff_attn_kernel, tq=tq, lam_init=lam_init),
        grid=(batch, DIFF_HEADS, nq),
        in_specs=[pl.BlockSpec((tq, LANES), lambda b, h, i: (b * nq + i, qb + h)),
                  pl.BlockSpec((seq, LANES), lambda b, h, i: (b, kb + h)),
                  pl.BlockSpec((seq, LANES), lambda b, h, i: (b, vb + h)),
                  pl.BlockSpec((1, 2, tq, tq), lambda b, h, i: (h, 0, 0, 0)),
                  pl.BlockSpec((4, DIFF_DK), lambda b, h, i: (0, 0)),
                  pl.BlockSpec((1, DIFF_DV), lambda b, h, i: (0, 0))],
        out_specs=pl.BlockSpec((tq, DIFF_DV), lambda b, h, i: (b * nq + i, h)),
        out_shape=jax.ShapeDtypeStruct((batch * seq, DIFF_HEADS * DIFF_DV), BF16),
        scratch_shapes=[stat, stat, accs, stat, stat, accs],
        compiler_params=_cparams(("parallel", "parallel", "arbitrary")),
        name="diff_attn",
    )(z, z, z, tz, lam_vecs, subln_g)


def _mla_prep_kernel(cq_ref, ckv_ref, kr_ref, gq_ref, gkv_ref, wuq_ref, wukv_ref,
                     cos_ref, sa_ref, sb_ref, qm_ref, km_ref, vm_ref):
    def rope(x):
        return (x * cos_ref[...] + pltpu.roll(x, 96, 1) * sa_ref[...]
                + pltpu.roll(x, 32, 1) * sb_ref[...])

    cq = _rmsnorm_rows(cq_ref[...].astype(F32), gq_ref[...]).astype(BF16)
    q = jnp.dot(cq, wuq_ref[...], preferred_element_type=F32)
    ckv = _rmsnorm_rows(ckv_ref[...].astype(F32), gkv_ref[...]).astype(BF16)
    kv = jnp.dot(ckv, wukv_ref[...], preferred_element_type=F32)
    krr = rope(kr_ref[...].astype(F32)).astype(BF16)
    scale = (MLA_NOPE + MLA_ROPE) ** -0.5
    hw = 2 * LANES
    for h in range(MLA_HEADS):
        qm_ref[:, h * hw:h * hw + LANES] = (q[:, h * hw:h * hw + LANES] * scale).astype(BF16)
        qm_ref[:, h * hw + LANES:(h + 1) * hw] = (rope(q[:, h * hw + LANES:(h + 1) * hw]) * scale).astype(BF16)
        km_ref[:, h * hw:h * hw + LANES] = kv[:, h * hw:h * hw + LANES].astype(BF16)
        km_ref[:, h * hw + LANES:(h + 1) * hw] = krr
        vm_ref[:, h * MLA_DV:(h + 1) * MLA_DV] = kv[:, h * hw + LANES:(h + 1) * hw].astype(BF16)


def _mla_prep(z, gq, gkv, wuq, wukv, cos_t, sa_t, sb_t, *, seq, qkv_col):
    t = z.shape[0]
    tm = min(512, seq)
    ns = seq // tm
    hw = 2 * LANES
    cq_blk = (qkv_col + OFF_CQ) // MLA_Q_RANK
    ckv_blk = (qkv_col + OFF_CKV) // MLA_KV_RANK
    kr_blk = (qkv_col + OFF_KR) // LANES
    rope_spec = pl.BlockSpec((tm, LANES), lambda i: (i % ns, 0))
    return pl.pallas_call(
        _mla_prep_kernel,
        grid=(t // tm,),
        in_specs=[pl.BlockSpec((tm, MLA_Q_RANK), lambda i: (i, cq_blk)),
                  pl.BlockSpec((tm, MLA_KV_RANK), lambda i: (i, ckv_blk)),
                  pl.BlockSpec((tm, LANES), lambda i: (i, kr_blk)),
                  pl.BlockSpec((1, MLA_Q_RANK), lambda i: (0, 0)),
                  pl.BlockSpec((1, MLA_KV_RANK), lambda i: (0, 0)),
                  pl.BlockSpec((MLA_Q_RANK, MLA_HEADS * hw), lambda i: (0, 0)),
                  pl.BlockSpec((MLA_KV_RANK, MLA_HEADS * hw), lambda i: (0, 0)),
                  rope_spec, rope_spec, rope_spec],
        out_specs=[pl.BlockSpec((tm, MLA_HEADS * hw), lambda i: (i, 0)),
                   pl.BlockSpec((tm, MLA_HEADS * hw), lambda i: (i, 0)),
                   pl.BlockSpec((tm, MLA_HEADS * MLA_DV), lambda i: (i, 0))],
        out_shape=[jax.ShapeDtypeStruct((t, MLA_HEADS * hw), BF16),
                   jax.ShapeDtypeStruct((t, MLA_HEADS * hw), BF16),
                   jax.ShapeDtypeStruct((t, MLA_HEADS * MLA_DV), BF16)],
        compiler_params=_cparams(("parallel",)),
        name="mla_prep",
    )(z, z, z, gq, gkv, wuq, wukv, cos_t, sa_t, sb_t)


def _mla_attn_kernel(q_ref, k_ref, v_ref, o_ref, m, l, acc, *, tq):
    qi = pl.program_id(2)
    q = q_ref[...]

    kt, vt = _kv_tile(k_ref, v_ref, qi, tq)
    s = lax.dot_general(q, kt, _NT, preferred_element_type=F32)
    r = lax.broadcasted_iota(jnp.int32, s.shape, 0)
    c = lax.broadcasted_iota(jnp.int32, s.shape, 1)
    _attn_first(jnp.where(r >= c, s, NEG_INF), vt, m, l, acc)

    def body(j, carry):
        kt, vt = _kv_tile(k_ref, v_ref, j, tq)
        _attn_next(lax.dot_general(q, kt, _NT, preferred_element_type=F32), vt, m, l, acc)
        return carry

    lax.fori_loop(0, qi, body, 0)
    o_ref[...] = (acc[...] / l[...]).astype(o_ref.dtype)


def _mla_attention(qm, km, vm, *, batch, seq):
    tq = ATT_TQ
    nq = seq // tq
    hw = 2 * LANES
    return pl.pallas_call(
        functools.partial(_mla_attn_kernel, tq=tq),
        grid=(batch, MLA_HEADS, nq),
        in_specs=[pl.BlockSpec((tq, hw), lambda b, h, i: (b * nq + i, h)),
                  pl.BlockSpec((seq, hw), lambda b, h, i: (b, h)),
                  pl.BlockSpec((seq, MLA_DV), lambda b, h, i: (b, h))],
        out_specs=pl.BlockSpec((tq, MLA_DV), lambda b, h, i: (b * nq + i, h)),
        out_shape=jax.ShapeDtypeStruct((batch * seq, MLA_HEADS * MLA_DV), BF16),
        scratch_shapes=[pltpu.VMEM((tq, 1), F32), pltpu.VMEM((tq, 1), F32), pltpu.VMEM((tq, MLA_DV), F32)],
        compiler_params=_cparams(("parallel", "parallel", "arbitrary")),
        name="mla_attn",
    )(qm, km, vm)


def _swa_kernel(q0, q1, q2, q3, kc_ref, kp_ref, vc_ref, vp_ref, sw_ref, sink_ref, o_ref):
    n = pl.program_id(1)
    w = SWA_WINDOW
    kc, kp, vc, vp = kc_ref[...], kp_ref[...], vc_ref[...], vp_ref[...]
    lane = lax.broadcasted_iota(jnp.int32, (w, LANES), 1)
    col = lax.broadcasted_iota(jnp.int32, (w, 2 * w), 1)
    first = jnp.where((col < w) & (n == 0), NEG_INF, 0.0)
    for g, q_ref in enumerate((q0, q1, q2, q3)):
        qs = q_ref[...] * (SWA_DH ** -0.5)
        zero = jnp.zeros_like(qs)
        outs = []
        for kv in range(SWA_KV_HEADS):
            head = kv * SWA_GROUP + g
            qm = jnp.where((lane >= kv * SWA_DH) & (lane < (kv + 1) * SWA_DH), qs, zero)
            s = jnp.concatenate([lax.dot_general(qm, kp, _NT, preferred_element_type=F32),
                                 lax.dot_general(qm, kc, _NT, preferred_element_type=F32)], axis=1)
            s = s + sw_ref[head] + first
            sink = sink_ref[head]
            m = jnp.maximum(jnp.max(s, axis=-1, keepdims=True), sink)
            p = jnp.exp(s - m)
            denom = jnp.sum(p, axis=-1, keepdims=True) + jnp.exp(sink - m)
            pb = p.astype(BF16)
            o = (jnp.dot(pb[:, :w], vp, preferred_element_type=F32)
                 + jnp.dot(pb[:, w:], vc, preferred_element_type=F32))
            outs.append(o / denom)
        o_ref[:, g * LANES:(g + 1) * LANES] = jnp.where(lane < SWA_DH, outs[0], outs[1]).astype(o_ref.dtype)


def _swa_attention(z, sw, sinks, *, batch, seq, qkv_blk):
    w = SWA_WINDOW
    nb = seq // w
    qb = qkv_blk + OFF_SQ // LANES
    kb = qkv_blk + OFF_SK // LANES
    vb = qkv_blk + OFF_SV // LANES

    def cur(col):
        return pl.BlockSpec((w, LANES), lambda b, n: (b * nb + n, col))

    def prev(col):
        return pl.BlockSpec((w, LANES), lambda b, n: (b * nb + jnp.maximum(n - 1, 0), col))

    return pl.pallas_call(
        _swa_kernel,
        grid=(batch, nb),
        in_specs=[cur(qb), cur(qb + 1), cur(qb + 2), cur(qb + 3),
                  cur(kb), prev(kb), cur(vb), prev(vb),
                  pl.BlockSpec((SWA_Q_HEADS, w, 2 * w), lambda b, n: (0, 0, 0)),
                  pl.BlockSpec(memory_space=pltpu.SMEM)],
        out_specs=pl.BlockSpec((w, SWA_Q_HEADS * SWA_DH), lambda b, n: (b * nb + n, 0)),
        out_shape=jax.ShapeDtypeStruct((batch * seq, SWA_Q_HEADS * SWA_DH), BF16),
        compiler_params=_cparams(("parallel", "arbitrary")),
        name="swa_attn",
    )(z, z, z, z, z, z, z, z, sw, sinks)


def _moba_kernel(q_ref, k_ref, v_ref, tz_ref, o_ref, kmean, selm, m, l, acc, *, tq, nblk):
    qi = pl.program_id(2)

    @pl.when(qi == 0)
    def _():
        kmean[...] = jnp.zeros_like(kmean)
        for n in range(nblk):
            kmean[n:n + 1, :] = jnp.mean(k_ref[n * tq:(n + 1) * tq, :].astype(F32), axis=0, keepdims=True)

    q = q_ref[...]
    km = kmean[...]
    km_hi = km.astype(BF16)
    km_lo = (km - km_hi.astype(F32)).astype(BF16)
    gate = (lax.dot_general(q, km_hi, _NT, preferred_element_type=F32)
            + lax.dot_general(q, km_lo, _NT, preferred_element_type=F32))
    lane = lax.broadcasted_iota(jnp.int32, gate.shape, 1)
    valid = lane < qi
    g = jnp.where(valid, gate, NEG_INF)
    rank = jnp.zeros(gate.shape, jnp.int32)
    for n in range(nblk):
        col = g[:, n:n + 1]
        ahead = (col > g) | ((col == g) & (lane > n))
        rank = rank + jnp.where(ahead, 1, 0)
    selm[...] = jnp.where(valid & (rank < MOBA_TOPK), 0.0, NEG_INF)

    scale = MOBA_DH ** -0.5

    def scores(kt):
        return lax.dot_general(q, kt, _NT, preferred_element_type=F32) * scale

    def sel_col(n):
        return jnp.sum(jnp.where(lane == n, selm[...], 0.0), axis=-1, keepdims=True)

    kt, vt = _kv_tile(k_ref, v_ref, qi, tq)
    _attn_first(scores(kt) + tz_ref[0, 0], vt, m, l, acc)

    @pl.when(qi >= 1)
    def _():
        kt, vt = _kv_tile(k_ref, v_ref, qi - 1, tq)
        _attn_next(scores(kt) + tz_ref[0, 1] + sel_col(qi - 1), vt, m, l, acc)

    def body(j, carry):
        kt, vt = _kv_tile(k_ref, v_ref, j, tq)
        _attn_next(scores(kt) + sel_col(j), vt, m, l, acc)
        return carry

    lax.fori_loop(0, qi - 1, body, 0)
    o_ref[...] = (acc[...] / l[...]).astype(o_ref.dtype)


def _moba_attention(z, tz, *, batch, seq, qkv_blk):
    tq = MOBA_BLOCK
    assert tq == ATT_TQ and seq % tq == 0
    nq = seq // tq
    assert nq <= LANES
    qb, kb, vb = (qkv_blk + off // LANES for off in (OFF_MQ, OFF_MK, OFF_MV))
    return pl.pallas_call(
        functools.partial(_moba_kernel, tq=tq, nblk=nq),
        grid=(batch, MOBA_HEADS, nq),
        in_specs=[pl.BlockSpec((tq, LANES), lambda b, h, i: (b * nq + i, qb + h)),
                  pl.BlockSpec((seq, LANES), lambda b, h, i: (b, kb + h)),
                  pl.BlockSpec((seq, LANES), lambda b, h, i: (b, vb + h)),
                  pl.BlockSpec((1, 2, tq, tq), lambda b, h, i: (BIAS_D0 + h, 0, 0, 0))],
        out_specs=pl.BlockSpec((tq, MOBA_DH), lambda b, h, i: (b * nq + i, h)),
        out_shape=jax.ShapeDtypeStruct((batch * seq, MOBA_HEADS * MOBA_DH), BF16),
        scratch_shapes=[pltpu.VMEM((LANES, MOBA_DH), F32), pltpu.VMEM((tq, LANES), F32),
                        pltpu.VMEM((tq, 1), F32), pltpu.VMEM((tq, 1), F32), pltpu.VMEM((tq, MOBA_DH), F32)],
        compiler_params=_cparams(("parallel", "parallel", "arbitrary")),
        name="moba_attn",
    )(z, z, z, tz)


def _merge_kernel(x_ref, a_ref, b_ref, c_ref, d_ref, g0, g1, g2, g3, wb_ref, wo_ref, o_ref):
    merged = None
    for i, (br, gr) in enumerate(((a_ref, g0), (b_ref, g1), (c_ref, g2), (d_ref, g3))):
        y = jnp.dot(br[...], wb_ref[i], preferred_element_type=F32)
        gate = 1.0 / (1.0 + jnp.exp(-gr[...].astype(F32)))
        merged = gate * y if merged is None else merged + gate * y
    o_ref[...] = x_ref[...] + jnp.dot(merged.astype(BF16), wo_ref[...], preferred_element_type=F32)


def _merge(x, z, branches, wb, wo):
    t, d = x.shape
    tm = min(256, t)
    row = lambda i: (i, 0)
    const1 = pl.Buffered(1)
    return pl.pallas_call(
        _merge_kernel,
        grid=(t // tm,),
        in_specs=[pl.BlockSpec((tm, d), row)]
                 + [pl.BlockSpec((tm, BRANCH_W), row)] * N_BRANCH
                 + [pl.BlockSpec((tm, d), functools.partial(lambda i, k: (i, k), k=k)) for k in range(N_BRANCH)]
                 + [pl.BlockSpec((N_BRANCH, BRANCH_W, d), lambda i: (0, 0, 0), pipeline_mode=const1),
                    pl.BlockSpec((d, d), lambda i: (0, 0), pipeline_mode=const1)],
        out_specs=pl.BlockSpec((tm, d), row),
        out_shape=jax.ShapeDtypeStruct((t, d), F32),
        compiler_params=_cparams(("parallel",)),
        name="merge_out",
    )(x, *branches, z, z, z, z, wb, wo)


def _ffn_up_kernel(x_ref, xh_ref, g_ref, wa_ref, wv_ref, cw_ref, cb_ref, o_ref, h_ref, abuf,
                   *, rows, halo, tiles_per_seq):
    i = pl.program_id(0)
    tm = x_ref.shape[0]

    @pl.when(pl.program_id(1) == 0)
    def _():
        keep = jnp.where(i % tiles_per_seq == 0, 0.0, 1.0)
        h_ref[0:halo, :] = (_rmsnorm_rows(xh_ref[...], g_ref[...]) * keep).astype(BF16)

        def body(c, carry):
            r = pl.multiple_of(c * rows, rows)
            h_ref[pl.ds(halo + r, rows), :] = _rmsnorm_rows(x_ref[pl.ds(r, rows), :], g_ref[...]).astype(BF16)
            return carry
        lax.fori_loop(0, tm // rows, body, 0)

    abuf[...] = jnp.dot(h_ref[...], wa_ref[...], preferred_element_type=F32)
    val = jnp.dot(h_ref[halo:, :], wv_ref[...], preferred_element_type=F32)
    cw = cw_ref[...]
    a = cb_ref[...] + cw[0:1] * abuf[pl.ds(halo - 2, tm), :]
    a = a + cw[1:2] * abuf[pl.ds(halo - 1, tm), :]
    a = a + cw[2:3] * abuf[pl.ds(halo, tm), :]
    gelu = 0.5 * a * (1.0 + lax.erf(a * math.sqrt(0.5)))
    o_ref[...] = (gelu * val).astype(o_ref.dtype)


def _ffn_up(x, g, w_up, conv_w, conv_b, *, seq):
    t, d = x.shape
    f = conv_w.shape[1]
    tm = min(1024, seq)
    tn = min(512, f)
    halo = 16
    rows = min(256, tm)
    nj = f // tn
    return pl.pallas_call(
        functools.partial(_ffn_up_kernel, rows=rows, halo=halo, tiles_per_seq=seq // tm),
        grid=(t // tm, nj),
        in_specs=[pl.BlockSpec((tm, d), lambda i, j: (i, 0)),
                  pl.BlockSpec((halo, d), lambda i, j: (jnp.maximum(i * (tm // halo) - 1, 0), 0)),
                  pl.BlockSpec((1, d), lambda i, j: (0, 0)),
                  pl.BlockSpec((d, tn), lambda i, j: (0, j)),
                  pl.BlockSpec((d, tn), lambda i, j: (0, nj + j)),
                  pl.BlockSpec((CONV_W, tn), lambda i, j: (0, j)),
                  pl.BlockSpec((1, tn), lambda i, j: (0, j))],
        out_specs=pl.BlockSpec((tm, tn), lambda i, j: (i, j)),
        out_shape=jax.ShapeDtypeStruct((t, f), BF16),
        scratch_shapes=[pltpu.VMEM((tm + halo, d), BF16), pltpu.VMEM((tm + halo, tn), F32)],
        compiler_params=_cparams(("parallel", "arbitrary")),
        name="ffn_up",
    )(x, x, g, w_up, w_up, conv_w, conv_b)


def _ffn_down_kernel(a_ref, w_ref, x_ref, gf_ref, o_ref, *, final):
    y = x_ref[...] + jnp.dot(a_ref[...], w_ref[...], preferred_element_type=F32)
    if final:
        y = _rmsnorm_rows(y, gf_ref[...])
    o_ref[...] = y


def _ffn_down(a, w, x, gf, *, final):
    t, d = x.shape
    f = a.shape[1]
    tm = min(256, t)
    return pl.pallas_call(
        functools.partial(_ffn_down_kernel, final=final),
        grid=(t // tm,),
        in_specs=[pl.BlockSpec((tm, f), lambda i: (i, 0)),
                  pl.BlockSpec((f, d), lambda i: (0, 0), pipeline_mode=pl.Buffered(1)),
                  pl.BlockSpec((tm, d), lambda i: (i, 0)),
                  pl.BlockSpec((1, d), lambda i: (0, 0))],
        out_specs=pl.BlockSpec((tm, d), lambda i: (i, 0)),
        out_shape=jax.ShapeDtypeStruct((t, d), F32),
        compiler_params=_cparams(("parallel",)),
        name="ffn_down",
    )(a, w, x, gf)


def _pack_w_in(w, d):
    o = 0
    parts = {}
    for name, width in (("aq", 512), ("ak", 512), ("av", 512), ("cq", 512), ("ckv", 256), ("kr", 64),
                        ("sq", 512), ("sk", 128), ("sv", 128), ("mq", 512), ("mk", 512), ("mv", 512),
                        ("zg", N_BRANCH * d)):
        parts[name] = w[:, o:o + width]
        o += width
    sq = parts["sq"].reshape(d, SWA_KV_HEADS, SWA_GROUP, SWA_DH).transpose(0, 2, 1, 3).reshape(d, 512)
    cols = [parts["zg"], parts["aq"], parts["ak"], parts["av"], parts["mq"], parts["mk"], parts["mv"],
            parts["cq"], parts["ckv"], parts["kr"], jnp.zeros((d, LANES - MLA_ROPE), w.dtype),
            sq, parts["sk"], parts["sv"], jnp.zeros((d, QKV_W - QKV_USED), w.dtype)]
    return jnp.concatenate(cols, axis=1).astype(BF16)


def _rope_tables(positions):
    half = MLA_ROPE // 2
    inv = ROPE_THETA ** (-jnp.arange(0, MLA_ROPE, 2, dtype=F32) / MLA_ROPE)
    ang = positions.astype(F32)[:, None] * inv[None, :]
    cos, sin = jnp.cos(ang), jnp.sin(ang)
    zh = jnp.zeros_like(cos)
    zr = jnp.zeros((positions.shape[0], LANES - MLA_ROPE), F32)
    cos_t = jnp.concatenate([cos, cos, zr], axis=1)
    sa_t = jnp.concatenate([-sin, zh, zr], axis=1)
    sb_t = jnp.concatenate([zh, sin, zr], axis=1)
    del half
    return cos_t, sa_t, sb_t


def kernel(x, positions, rel_bias, norm1_g, w_in, diff_lambda, diff_subln_g, mla_q_norm_g, mla_w_uq,
           mla_kv_norm_g, mla_w_ukv, swa_sinks, w_branch, w_out, norm2_g, ffn_w_up, ffn_conv_w,
           ffn_conv_b, ffn_w_down, final_norm_g):
    batch, seq, d = x.shape
    depth = w_in.shape[0]
    t = batch * seq
    gates_w = N_BRANCH * d
    assert gates_w % 1024 == 0 and seq % ATT_TQ == 0
    qkv_blk = gates_w // LANES

    tz, sw = _bias_tables(rel_bias)
    sw = sw[BIAS_C0:BIAS_C0 + SWA_Q_HEADS]
    cos_t, sa_t, sb_t = _rope_tables(positions)

    xf = x.reshape(t, d)
    for l in range(depth):
        lam_init = 0.8 - 0.6 * math.exp(-0.3 * l)
        w_in_l = _pack_w_in(w_in[l], d)
        wuq = jnp.pad(mla_w_uq[l].reshape(MLA_Q_RANK, MLA_HEADS, MLA_NOPE + MLA_ROPE),
                      ((0, 0), (0, 0), (0, 2 * LANES - MLA_NOPE - MLA_ROPE))
                      ).reshape(MLA_Q_RANK, MLA_HEADS * 2 * LANES).astype(BF16)
        wukv = mla_w_ukv[l].astype(BF16)
        wb = w_branch[l]
        wb_swa = wb[2].reshape(SWA_KV_HEADS, SWA_GROUP, SWA_DH, d).transpose(1, 0, 2, 3).reshape(BRANCH_W, d)
        wb = jnp.stack([wb[0], wb[1], wb_swa, wb[3]]).astype(BF16)
        wo = w_out[l].astype(BF16)
        w_up = ffn_w_up[l].astype(BF16)
        w_down = ffn_w_down[l].astype(BF16)

        z = _in_proj(xf, norm1_g[l].reshape(1, d), w_in_l)
        br_a = _diff_attention(z, tz, diff_lambda[l], diff_subln_g[l].reshape(1, DIFF_DV), lam_init,
                               batch=batch, seq=seq, qkv_blk=qkv_blk)
        qm, km, vm = _mla_prep(z, mla_q_norm_g[l].reshape(1, MLA_Q_RANK), mla_kv_norm_g[l].reshape(1, MLA_KV_RANK),
                               wuq, wukv, cos_t, sa_t, sb_t, seq=seq, qkv_col=gates_w)
        br_b = _mla_attention(qm, km, vm, batch=batch, seq=seq)
        br_c = _swa_attention(z, sw, swa_sinks[l], batch=batch, seq=seq, qkv_blk=qkv_blk)
        br_d = _moba_attention(z, tz, batch=batch, seq=seq, qkv_blk=qkv_blk)
        xf = _merge(xf, z, (br_a, br_b, br_c, br_d), wb, wo)
        act = _ffn_up(xf, norm2_g[l].reshape(1, d), w_up, ffn_conv_w[l], ffn_conv_b[l].reshape(1, -1), seq=seq)
        xf = _ffn_down(act, w_down, xf, final_norm_g.reshape(1, d), final=(l == depth - 1))
    return xf.reshape(batch, seq, d)
```

```python
import functools
import math

import numpy as np
import jax
import jax.numpy as jnp
from jax import lax
from jax.experimental import pallas as pl
from jax.experimental.pallas import tpu as pltpu

F32 = jnp.float32
BF16 = jnp.bfloat16

DIFF_HEADS = 4
DIFF_DK = 64
DIFF_DV = 128
MLA_HEADS = 4
MLA_Q_RANK = 512
MLA_KV_RANK = 256
MLA_NOPE = 128
MLA_ROPE = 64
MLA_DV = 128
ROPE_THETA = 10000.0
SWA_Q_HEADS = 8
SWA_KV_HEADS = 2
SWA_GROUP = SWA_Q_HEADS // SWA_KV_HEADS
SWA_DH = 64
SWA_WINDOW = 128
MOBA_HEADS = 4
MOBA_DH = 128
MOBA_BLOCK = 256
MOBA_TOPK = 3
REL_BUCKETS = 32
REL_MAX_DIST = 128
BIAS_HEADS = DIFF_HEADS + SWA_Q_HEADS + MOBA_HEADS
BIAS_C0 = DIFF_HEADS
BIAS_D0 = DIFF_HEADS + SWA_Q_HEADS
N_BRANCH = 4
BRANCH_W = 512
CONV_W = 3
EPS = 1e-6
NEG_INF = -1e30

LANES = 128
ATT_TQ = 512
BIAS_CORNER = 128
VMEM_LIMIT = 56 * 1024 * 1024

QKV_W = 5120
OFF_AQ, OFF_AK, OFF_AV = 0, 512, 1024
OFF_MQ, OFF_MK, OFF_MV = 1536, 2048, 2560
OFF_CQ, OFF_CKV, OFF_KR = 3072, 3584, 3840
OFF_SQ, OFF_SK, OFF_SV = 3968, 4480, 4608
QKV_USED = 4736

_NT = (((1,), (1,)), ((), ()))


def _t5_thresholds():
    n = np.arange(0, 2 * REL_MAX_DIST)
    max_exact = REL_BUCKETS // 2
    nf = np.maximum(n, 1).astype(np.float32)
    large = max_exact + (np.log(nf / np.float32(max_exact)) / np.float32(math.log(REL_MAX_DIST / max_exact))
                         * np.float32(REL_BUCKETS - max_exact)).astype(np.int32)
    bkt = np.where(n < max_exact, n, np.minimum(large, REL_BUCKETS - 1))
    return tuple(int(np.argmax(bkt >= b)) for b in range(1, REL_BUCKETS))


T5_THR = _t5_thresholds()


def _cparams(sem):
    return pltpu.CompilerParams(dimension_semantics=sem, vmem_limit_bytes=VMEM_LIMIT)


def _rmsnorm_rows(x, g):
    ms = jnp.mean(x * x, axis=-1, keepdims=True)
    return x * lax.rsqrt(ms + EPS) * g


def _in_proj_kernel(x_ref, g_ref, w_ref, z_ref, h_ref, *, rows):
    @pl.when(pl.program_id(1) == 0)
    def _():
        def body(c, carry):
            r = pl.multiple_of(c * rows, rows)
            h_ref[pl.ds(r, rows), :] = _rmsnorm_rows(x_ref[pl.ds(r, rows), :], g_ref[...]).astype(BF16)
            return carry
        lax.fori_loop(0, x_ref.shape[0] // rows, body, 0)

    z_ref[...] = jnp.dot(h_ref[...], w_ref[...], preferred_element_type=F32).astype(z_ref.dtype)


def _in_proj(x, g, w):
    t, d = x.shape
    nz = w.shape[1]
    tm = min(1024, t)
    tn = 1024
    rows = min(256, tm)
    return pl.pallas_call(
        functools.partial(_in_proj_kernel, rows=rows),
        grid=(t // tm, nz // tn),
        in_specs=[pl.BlockSpec((tm, d), lambda i, j: (i, 0)),
                  pl.BlockSpec((1, d), lambda i, j: (0, 0)),
                  pl.BlockSpec((d, tn), lambda i, j: (0, j))],
        out_specs=pl.BlockSpec((tm, tn), lambda i, j: (i, j)),
        out_shape=jax.ShapeDtypeStruct((t, nz), BF16),
        scratch_shapes=[pltpu.VMEM((tm, d), BF16)],
        compiler_params=_cparams(("parallel", "arbitrary")),
        name="in_proj",
    )(x, g, w)


def _bias_table_kernel(tab_ref, tzd_ref, tzc_ref, sw_ref, *, tq):
    h = pl.program_id(0)

    def lookup(n):
        val = jnp.full(n.shape, tab_ref[0, h], F32)
        for b in range(1, REL_BUCKETS):
            val = jnp.where(n >= T5_THR[b - 1], tab_ref[b, h], val)
        return val

    far = tab_ref[REL_BUCKETS - 1, h]
    r = lax.broadcasted_iota(jnp.int32, (tq, tq), 0)
    c = lax.broadcasted_iota(jnp.int32, (tq, tq), 1)
    rel = c - r
    tzd_ref[0] = jnp.where(rel >= 0, lookup(jnp.maximum(rel, 0)) - far, NEG_INF)
    r = lax.broadcasted_iota(jnp.int32, (BIAS_CORNER, BIAS_CORNER), 0)
    c = lax.broadcasted_iota(jnp.int32, (BIAS_CORNER, BIAS_CORNER), 1)
    tzc_ref[0] = lookup(BIAS_CORNER + c - r) - far
    r = lax.broadcasted_iota(jnp.int32, (SWA_WINDOW, 2 * SWA_WINDOW), 0)
    c = lax.broadcasted_iota(jnp.int32, (SWA_WINDOW, 2 * SWA_WINDOW), 1)
    rel = r - c + SWA_WINDOW
    valid = (rel >= 0) & (rel < SWA_WINDOW)
    sw_ref[0] = jnp.where(valid, lookup(jnp.maximum(rel, 0)), NEG_INF)


def _bias_tables(rel_bias):
    tq = ATT_TQ
    return pl.pallas_call(
        functools.partial(_bias_table_kernel, tq=tq),
        grid=(BIAS_HEADS,),
        in_specs=[pl.BlockSpec(memory_space=pltpu.SMEM)],
        out_specs=[pl.BlockSpec((1, tq, tq), lambda h: (h, 0, 0)),
                   pl.BlockSpec((1, BIAS_CORNER, BIAS_CORNER), lambda h: (h, 0, 0)),
                   pl.BlockSpec((1, SWA_WINDOW, 2 * SWA_WINDOW), lambda h: (h, 0, 0))],
        out_shape=[jax.ShapeDtypeStruct((BIAS_HEADS, tq, tq), F32),
                   jax.ShapeDtypeStruct((BIAS_HEADS, BIAS_CORNER, BIAS_CORNER), F32),
                   jax.ShapeDtypeStruct((BIAS_HEADS, SWA_WINDOW, 2 * SWA_WINDOW), F32)],
        compiler_params=_cparams(("arbitrary",)),
        name="bias_tables",
    )(rel_bias)


def _fold_rows(x, op):
    rows = x.shape[0]
    while rows > 8 and rows % 2 == 0:
        rows //= 2
        x = op(x[:rows], x[rows:])
    return x


def _col_max(x):
    return jnp.max(_fold_rows(x, jnp.maximum), axis=0, keepdims=True)


def _col_sum(x):
    return jnp.sum(_fold_rows(x, jnp.add), axis=0, keepdims=True)


def _attn_first(st, vt, m_ref, l_ref, acc_ref):
    m = _col_max(st)
    p = jnp.exp(st - m)
    m_ref[...] = m
    l_ref[...] = _col_sum(p)
    acc_ref[...] = jnp.dot(vt, p.astype(vt.dtype), preferred_element_type=F32)


def _attn_next(st, vt, m_ref, l_ref, acc_ref):
    m_old = m_ref[...]
    m_new = jnp.maximum(m_old, _col_max(st))
    alpha = jnp.exp(m_old - m_new)
    p = jnp.exp(st - m_new)
    m_ref[...] = m_new
    l_ref[...] = alpha * l_ref[...] + _col_sum(p)
    acc_ref[...] = alpha * acc_ref[...] + jnp.dot(vt, p.astype(vt.dtype), preferred_element_type=F32)


def _transpose_values(v_ref, vt_ref, tk):
    def body(c, carry):
        start = pl.multiple_of(c * tk, tk)
        vt_ref[c] = v_ref[pl.ds(start, tk), :].astype(F32).T.astype(vt_ref.dtype)
        return carry
    lax.fori_loop(0, vt_ref.shape[0], body, 0)


def _k_tile(k_ref, j, tk):
    return k_ref[pl.ds(pl.multiple_of(j * tk, tk), tk), :]


def _add_corner(st, corner):
    n = corner.shape[0]
    tk = st.shape[0]
    bottom = st[tk - n:]
    bottom = jnp.concatenate([bottom[:, :n] + corner, bottom[:, n:]], axis=1)
    return jnp.concatenate([st[:tk - n], bottom], axis=0)


def _diff_attn_kernel(q_ref, k_ref, v_ref, tzd_ref, tzc_ref, lam_ref, g_ref, o_ref,
                      vt_ref, m, l, acc, *, tq, lam_init):
    qi = pl.program_id(1)

    @pl.when(qi == 0)
    def _():
        _transpose_values(v_ref, vt_ref, tq)

    lane = lax.broadcasted_iota(jnp.int32, (tq, LANES), 1)
    qmaps = []
    for h in range(DIFF_HEADS):
        qs = q_ref[:, h * LANES:(h + 1) * LANES] * (DIFF_DK ** -0.5)
        zero = jnp.zeros_like(qs)
        qmaps.append(jnp.where(lane < DIFF_DK, qs, zero))
        qmaps.append(jnp.where(lane >= DIFF_DK, qs, zero))

    def step(j, kind, update):
        kt = _k_tile(k_ref, j, tq)
        vt = vt_ref[j]
        for c in range(2 * DIFF_HEADS):
            h = c // 2
            st = lax.dot_general(kt[:, h * LANES:(h + 1) * LANES], qmaps[c], _NT, preferred_element_type=F32)
            if kind == "diag":
                st = st + tzd_ref[h]
            elif kind == "prev":
                st = _add_corner(st, tzc_ref[h])
            update(st, vt[h * DIFF_DV:(h + 1) * DIFF_DV, :], m.at[c], l.at[c], acc.at[c])

    step(qi, "diag", _attn_first)

    @pl.when(qi >= 1)
    def _():
        step(qi - 1, "prev", _attn_next)

    def body(j, carry):
        step(j, "far", _attn_next)
        return carry

    lax.fori_loop(0, qi - 1, body, 0)

    lv = lam_ref[...]
    e1 = jnp.exp(jnp.sum(lv[0:1] * lv[1:2], axis=-1, keepdims=True))
    e2 = jnp.exp(jnp.sum(lv[2:3] * lv[3:4], axis=-1, keepdims=True))
    lam = e1 - e2 + lam_init
    for h in range(DIFF_HEADS):
        ot = acc[2 * h] / l[2 * h] - lam * (acc[2 * h + 1] / l[2 * h + 1])
        ot = ot * lax.rsqrt(jnp.mean(ot * ot, axis=0, keepdims=True) + EPS)
        o_ref[:, h * DIFF_DV:(h + 1) * DIFF_DV] = (ot.T * g_ref[...] * (1.0 - lam_init)).astype(o_ref.dtype)


def _diff_attention(z, tzd, tzc, lam_vecs, subln_g, lam_init, *, batch, seq, qkv_col):
    tq = ATT_TQ
    nq = seq // tq
    hw = DIFF_HEADS * LANES
    qb, kb, vb = ((qkv_col + off) // hw for off in (OFF_AQ, OFF_AK, OFF_AV))
    chains = 2 * DIFF_HEADS
    return pl.pallas_call(
        functools.partial(_diff_attn_kernel, tq=tq, lam_init=lam_init),
        grid=(batch, nq),
        in_specs=[pl.BlockSpec((tq, hw), lambda b, i: (b * nq + i, qb)),
                  pl.BlockSpec((seq, hw), lambda b, i: (b, kb)),
                  pl.BlockSpec((seq, hw), lambda b, i: (b, vb)),
                  pl.BlockSpec((DIFF_HEADS, tq, tq), lambda b, i: (0, 0, 0)),
                  pl.BlockSpec((DIFF_HEADS, BIAS_CORNER, BIAS_CORNER), lambda b, i: (0, 0, 0)),
                  pl.BlockSpec((4, DIFF_DK), lambda b, i: (0, 0)),
                  pl.BlockSpec((1, DIFF_DV), lambda b, i: (0, 0))],
        out_specs=pl.BlockSpec((tq, DIFF_HEADS * DIFF_DV), lambda b, i: (b * nq + i, 0)),
        out_shape=jax.ShapeDtypeStruct((batch * seq, DIFF_HEADS * DIFF_DV), BF16),
        scratch_shapes=[pltpu.VMEM((nq, DIFF_HEADS * DIFF_DV, tq), BF16),
                        pltpu.VMEM((chains, 1, tq), F32), pltpu.VMEM((chains, 1, tq), F32),
                        pltpu.VMEM((chains, DIFF_DV, tq), F32)],
        compiler_params=_cparams(("parallel", "arbitrary")),
        name="diff_attn",
    )(z, z, z, tzd, tzc, lam_vecs, subln_g)


def _mla_prep_kernel(cq_ref, ckv_ref, kr_ref, gq_ref, gkv_ref, wuq_ref, wukv_ref,
                     cos_ref, sa_ref, sb_ref, qm_ref, km_ref, vm_ref):
    def rope(x):
        return (x * cos_ref[...] + pltpu.roll(x, 96, 1) * sa_ref[...]
                + pltpu.roll(x, 32, 1) * sb_ref[...])

    cq = _rmsnorm_rows(cq_ref[...].astype(F32), gq_ref[...]).astype(BF16)
    q = jnp.dot(cq, wuq_ref[...], preferred_element_type=F32)
    ckv = _rmsnorm_rows(ckv_ref[...].astype(F32), gkv_ref[...]).astype(BF16)
    kv = jnp.dot(ckv, wukv_ref[...], preferred_element_type=F32)
    krr = rope(kr_ref[...].astype(F32)).astype(BF16)
    scale = (MLA_NOPE + MLA_ROPE) ** -0.5
    hw = 2 * LANES
    for h in range(MLA_HEADS):
        qm_ref[:, h * hw:h * hw + LANES] = (q[:, h * hw:h * hw + LANES] * scale).astype(BF16)
        qm_ref[:, h * hw + LANES:(h + 1) * hw] = (rope(q[:, h * hw + LANES:(h + 1) * hw]) * scale).astype(BF16)
        km_ref[:, h * hw:h * hw + LANES] = kv[:, h * hw:h * hw + LANES].astype(BF16)
        km_ref[:, h * hw + LANES:(h + 1) * hw] = krr
        vm_ref[:, h * MLA_DV:(h + 1) * MLA_DV] = kv[:, h * hw + LANES:(h + 1) * hw].astype(BF16)


def _mla_prep(z, gq, gkv, wuq, wukv, cos_t, sa_t, sb_t, *, seq, qkv_col):
    t = z.shape[0]
    tm = min(512, seq)
    ns = seq // tm
    hw = 2 * LANES
    cq_blk = (qkv_col + OFF_CQ) // MLA_Q_RANK
    ckv_blk = (qkv_col + OFF_CKV) // MLA_KV_RANK
    kr_blk = (qkv_col + OFF_KR) // LANES
    rope_spec = pl.BlockSpec((tm, LANES), lambda i: (i % ns, 0))
    return pl.pallas_call(
        _mla_prep_kernel,
        grid=(t // tm,),
        in_specs=[pl.BlockSpec((tm, MLA_Q_RANK), lambda i: (i, cq_blk)),
                  pl.BlockSpec((tm, MLA_KV_RANK), lambda i: (i, ckv_blk)),
                  pl.BlockSpec((tm, LANES), lambda i: (i, kr_blk)),
                  pl.BlockSpec((1, MLA_Q_RANK), lambda i: (0, 0)),
                  pl.BlockSpec((1, MLA_KV_RANK), lambda i: (0, 0)),
                  pl.BlockSpec((MLA_Q_RANK, MLA_HEADS * hw), lambda i: (0, 0)),
                  pl.BlockSpec((MLA_KV_RANK, MLA_HEADS * hw), lambda i: (0, 0)),
                  rope_spec, rope_spec, rope_spec],
        out_specs=[pl.BlockSpec((tm, MLA_HEADS * hw), lambda i: (i, 0)),
                   pl.BlockSpec((tm, MLA_HEADS * hw), lambda i: (i, 0)),
                   pl.BlockSpec((tm, MLA_HEADS * MLA_DV), lambda i: (i, 0))],
        out_shape=[jax.ShapeDtypeStruct((t, MLA_HEADS * hw), BF16),
                   jax.ShapeDtypeStruct((t, MLA_HEADS * hw), BF16),
                   jax.ShapeDtypeStruct((t, MLA_HEADS * MLA_DV), BF16)],
        compiler_params=_cparams(("parallel",)),
        name="mla_prep",
    )(z, z, z, gq, gkv, wuq, wukv, cos_t, sa_t, sb_t)


def _mla_attn_kernel(q_ref, k_ref, v_ref, o_ref, vt_ref, m, l, acc, *, tq):
    qi = pl.program_id(1)
    hw = 2 * LANES

    @pl.when(qi == 0)
    def _():
        _transpose_values(v_ref, vt_ref, tq)

    def step(j, causal, update):
        kt = _k_tile(k_ref, j, tq)
        vt = vt_ref[j]
        for h in range(MLA_HEADS):
            st = lax.dot_general(kt[:, h * hw:(h + 1) * hw], q_ref[:, h * hw:(h + 1) * hw], _NT,
                                 preferred_element_type=F32)
            if causal:
                r = lax.broadcasted_iota(jnp.int32, st.shape, 0)
                c = lax.broadcasted_iota(jnp.int32, st.shape, 1)
                st = jnp.where(r <= c, st, NEG_INF)
            update(st, vt[h * MLA_DV:(h + 1) * MLA_DV, :], m.at[h], l.at[h], acc.at[h])

    step(qi, True, _attn_first)

    def body(j, carry):
        step(j, False, _attn_next)
        return carry

    lax.fori_loop(0, qi, body, 0)
    for h in range(MLA_HEADS):
        o_ref[:, h * MLA_DV:(h + 1) * MLA_DV] = (acc[h] / l[h]).T.astype(o_ref.dtype)


def _mla_attention(qm, km, vm, *, batch, seq):
    tq = ATT_TQ
    nq = seq // tq
    hw = 2 * LANES
    return pl.pallas_call(
        functools.partial(_mla_attn_kernel, tq=tq),
        grid=(batch, nq),
        in_specs=[pl.BlockSpec((tq, MLA_HEADS * hw), lambda b, i: (b * nq + i, 0)),
                  pl.BlockSpec((seq, MLA_HEADS * hw), lambda b, i: (b, 0)),
                  pl.BlockSpec((seq, MLA_HEADS * MLA_DV), lambda b, i: (b, 0))],
        out_specs=pl.BlockSpec((tq, MLA_HEADS * MLA_DV), lambda b, i: (b * nq + i, 0)),
        out_shape=jax.ShapeDtypeStruct((batch * seq, MLA_HEADS * MLA_DV), BF16),
        scratch_shapes=[pltpu.VMEM((nq, MLA_HEADS * MLA_DV, tq), BF16),
                        pltpu.VMEM((MLA_HEADS, 1, tq), F32), pltpu.VMEM((MLA_HEADS, 1, tq), F32),
                        pltpu.VMEM((MLA_HEADS, MLA_DV, tq), F32)],
        compiler_params=_cparams(("parallel", "arbitrary")),
        name="mla_attn",
    )(qm, km, vm)


def _swa_kernel(q0, q1, q2, q3, kc_ref, kp_ref, vc_ref, vp_ref, sw_ref, sink_ref, o_ref):
    n = pl.program_id(1)
    w = SWA_WINDOW
    kc, kp, vc, vp = kc_ref[...], kp_ref[...], vc_ref[...], vp_ref[...]
    lane = lax.broadcasted_iota(jnp.int32, (w, LANES), 1)
    col = lax.broadcasted_iota(jnp.int32, (w, 2 * w), 1)
    first = jnp.where((col < w) & (n == 0), NEG_INF, 0.0)
    for g, q_ref in enumerate((q0, q1, q2, q3)):
        qs = q_ref[...] * (SWA_DH ** -0.5)
        zero = jnp.zeros_like(qs)
        outs = []
        for kv in range(SWA_KV_HEADS):
            head = kv * SWA_GROUP + g
            qm = jnp.where((lane >= kv * SWA_DH) & (lane < (kv + 1) * SWA_DH), qs, zero)
            s = jnp.concatenate([lax.dot_general(qm, kp, _NT, preferred_element_type=F32),
                                 lax.dot_general(qm, kc, _NT, preferred_element_type=F32)], axis=1)
            s = s + sw_ref[head] + first
            sink = sink_ref[head]
            m = jnp.maximum(jnp.max(s, axis=-1, keepdims=True), sink)
            p = jnp.exp(s - m)
            denom = jnp.sum(p, axis=-1, keepdims=True) + jnp.exp(sink - m)
            pb = p.astype(BF16)
            o = (jnp.dot(pb[:, :w], vp, preferred_element_type=F32)
                 + jnp.dot(pb[:, w:], vc, preferred_element_type=F32))
            outs.append(o / denom)
        o_ref[:, g * LANES:(g + 1) * LANES] = jnp.where(lane < SWA_DH, outs[0], outs[1]).astype(o_ref.dtype)


def _swa_attention(z, sw, sinks, *, batch, seq, qkv_blk):
    w = SWA_WINDOW
    nb = seq // w
    qb = qkv_blk + OFF_SQ // LANES
    kb = qkv_blk + OFF_SK // LANES
    vb = qkv_blk + OFF_SV // LANES

    def cur(col):
        return pl.BlockSpec((w, LANES), lambda b, n: (b * nb + n, col))

    def prev(col):
        return pl.BlockSpec((w, LANES), lambda b, n: (b * nb + jnp.maximum(n - 1, 0), col))

    return pl.pallas_call(
        _swa_kernel,
        grid=(batch, nb),
        in_specs=[cur(qb), cur(qb + 1), cur(qb + 2), cur(qb + 3),
                  cur(kb), prev(kb), cur(vb), prev(vb),
                  pl.BlockSpec((SWA_Q_HEADS, w, 2 * w), lambda b, n: (0, 0, 0)),
                  pl.BlockSpec(memory_space=pltpu.SMEM)],
        out_specs=pl.BlockSpec((w, SWA_Q_HEADS * SWA_DH), lambda b, n: (b * nb + n, 0)),
        out_shape=jax.ShapeDtypeStruct((batch * seq, SWA_Q_HEADS * SWA_DH), BF16),
        compiler_params=_cparams(("parallel", "arbitrary")),
        name="swa_attn",
    )(z, z, z, z, z, z, z, z, sw, sinks)


def _moba_kernel(q_ref, k_ref, v_ref, tzd_ref, tzc_ref, o_ref, vt_ref, kmean, selm, m, l, acc, *, tq, nblk):
    qi = pl.program_id(1)
    dh = MOBA_DH
    blk = MOBA_BLOCK

    @pl.when(qi == 0)
    def _():
        _transpose_values(v_ref, vt_ref, tq)
        kmean[...] = jnp.zeros_like(kmean)
        for n in range(nblk):
            km = jnp.mean(k_ref[n * blk:(n + 1) * blk, :].astype(F32), axis=0, keepdims=True)
            for h in range(MOBA_HEADS):
                kmean[h, n:n + 1, :] = km[:, h * dh:(h + 1) * dh]

    in_b = lax.broadcasted_iota(jnp.int32, (1, tq), 1) >= blk
    own = 2 * qi + jnp.where(in_b, 1, 0)
    for h in range(MOBA_HEADS):
        q = q_ref[:, h * dh:(h + 1) * dh]
        km = kmean[h]
        km_hi = km.astype(BF16)
        km_lo = (km - km_hi.astype(F32)).astype(BF16)
        gate = (lax.dot_general(km_hi, q, _NT, preferred_element_type=F32)
                + lax.dot_general(km_lo, q, _NT, preferred_element_type=F32))
        row = lax.broadcasted_iota(jnp.int32, gate.shape, 0)
        valid = row < own
        g = jnp.where(valid, gate, NEG_INF)
        rank = jnp.zeros(gate.shape, jnp.int32)
        for n in range(nblk):
            other = g[n:n + 1, :]
            ahead = (other > g) | ((other == g) & (row > n))
            rank = rank + jnp.where(ahead, 1, 0)
        selm[h] = jnp.where(valid & (rank < MOBA_TOPK), 0.0, NEG_INF)

    scale = dh ** -0.5

    def step(j, kind, update):
        kt = _k_tile(k_ref, j, tq)
        vt = vt_ref[j]
        for h in range(MOBA_HEADS):
            st = lax.dot_general(kt[:, h * dh:(h + 1) * dh], q_ref[:, h * dh:(h + 1) * dh], _NT,
                                 preferred_element_type=F32) * scale
            sel_a = selm[h, pl.ds(2 * j, 1), :]
            if kind == "diag":
                st = st + tzd_ref[h]
                st = jnp.concatenate([st[:blk] + jnp.where(in_b, sel_a, 0.0), st[blk:]], axis=0)
            else:
                if kind == "prev":
                    st = _add_corner(st, tzc_ref[h])
                st = jnp.concatenate([st[:blk] + sel_a, st[blk:] + selm[h, pl.ds(2 * j + 1, 1), :]], axis=0)
            update(st, vt[h * dh:(h + 1) * dh, :], m.at[h], l.at[h], acc.at[h])

    step(qi, "diag", _attn_first)

    @pl.when(qi >= 1)
    def _():
        step(qi - 1, "prev", _attn_next)

    def body(j, carry):
        step(j, "far", _attn_next)
        return carry

    lax.fori_loop(0, qi - 1, body, 0)
    for h in range(MOBA_HEADS):
        o_ref[:, h * dh:(h + 1) * dh] = (acc[h] / l[h]).T.astype(o_ref.dtype)


def _moba_attention(z, tzd, tzc, *, batch, seq, qkv_col):
    tq = ATT_TQ
    assert tq == 2 * MOBA_BLOCK and seq % tq == 0
    nq = seq // tq
    nblk = seq // MOBA_BLOCK
    nrow = -(-nblk // 8) * 8
    hw = MOBA_HEADS * MOBA_DH
    qb, kb, vb = ((qkv_col + off) // hw for off in (OFF_MQ, OFF_MK, OFF_MV))
    hb = BIAS_D0 // MOBA_HEADS
    return pl.pallas_call(
        functools.partial(_moba_kernel, tq=tq, nblk=nblk),
        grid=(batch, nq),
        in_specs=[pl.BlockSpec((tq, hw), lambda b, i: (b * nq + i, qb)),
                  pl.BlockSpec((seq, hw), lambda b, i: (b, kb)),
                  pl.BlockSpec((seq, hw), lambda b, i: (b, vb)),
                  pl.BlockSpec((MOBA_HEADS, tq, tq), lambda b, i: (hb, 0, 0)),
                  pl.BlockSpec((MOBA_HEADS, BIAS_CORNER, BIAS_CORNER), lambda b, i: (hb, 0, 0))],
        out_specs=pl.BlockSpec((tq, hw), lambda b, i: (b * nq + i, 0)),
        out_shape=jax.ShapeDtypeStruct((batch * seq, hw), BF16),
        scratch_shapes=[pltpu.VMEM((nq, hw, tq), BF16), pltpu.VMEM((MOBA_HEADS, nrow, MOBA_DH), F32),
                        pltpu.VMEM((MOBA_HEADS, nrow, tq), F32),
                        pltpu.VMEM((MOBA_HEADS, 1, tq), F32), pltpu.VMEM((MOBA_HEADS, 1, tq), F32),
                        pltpu.VMEM((MOBA_HEADS, MOBA_DH, tq), F32)],
        compiler_params=_cparams(("parallel", "arbitrary")),
        name="moba_attn",
    )(z, z, z, tzd, tzc)


def _merge_kernel(x_ref, a_ref, b_ref, c_ref, d_ref, g0, g1, g2, g3, wb_ref, wo_ref, o_ref):
    merged = None
    for i, (br, gr) in enumerate(((a_ref, g0), (b_ref, g1), (c_ref, g2), (d_ref, g3))):
        y = jnp.dot(br[...], wb_ref[i], preferred_element_type=F32)
        gate = 1.0 / (1.0 + jnp.exp(-gr[...].astype(F32)))
        merged = gate * y if merged is None else merged + gate * y
    o_ref[...] = x_ref[...] + jnp.dot(merged.astype(BF16), wo_ref[...], preferred_element_type=F32)


def _merge(x, z, branches, wb, wo):
    t, d = x.shape
    tm = min(256, t)
    row = lambda i: (i, 0)
    const1 = pl.Buffered(1)
    return pl.pallas_call(
        _merge_kernel,
        grid=(t // tm,),
        in_specs=[pl.BlockSpec((tm, d), row)]
                 + [pl.BlockSpec((tm, BRANCH_W), row)] * N_BRANCH
                 + [pl.BlockSpec((tm, d), functools.partial(lambda i, k: (i, k), k=k)) for k in range(N_BRANCH)]
                 + [pl.BlockSpec((N_BRANCH, BRANCH_W, d), lambda i: (0, 0, 0), pipeline_mode=const1),
                    pl.BlockSpec((d, d), lambda i: (0, 0), pipeline_mode=const1)],
        out_specs=pl.BlockSpec((tm, d), row),
        out_shape=jax.ShapeDtypeStruct((t, d), F32),
        compiler_params=_cparams(("parallel",)),
        name="merge_out",
    )(x, *branches, z, z, z, z, wb, wo)


def _ffn_up_kernel(x_ref, xh_ref, g_ref, wa_ref, wv_ref, cw_ref, cb_ref, o_ref, h_ref, abuf,
                   *, rows, halo, tiles_per_seq):
    i = pl.program_id(0)
    tm = x_ref.shape[0]

    @pl.when(pl.program_id(1) == 0)
    def _():
        keep = jnp.where(i % tiles_per_seq == 0, 0.0, 1.0)
        h_ref[0:halo, :] = (_rmsnorm_rows(xh_ref[...], g_ref[...]) * keep).astype(BF16)

        def body(c, carry):
            r = pl.multiple_of(c * rows, rows)
            h_ref[pl.ds(halo + r, rows), :] = _rmsnorm_rows(x_ref[pl.ds(r, rows), :], g_ref[...]).astype(BF16)
            return carry
        lax.fori_loop(0, tm // rows, body, 0)

    abuf[...] = jnp.dot(h_ref[...], wa_ref[...], preferred_element_type=F32)
    val = jnp.dot(h_ref[halo:, :], wv_ref[...], preferred_element_type=F32)
    cw = cw_ref[...]
    a = cb_ref[...] + cw[0:1] * abuf[pl.ds(halo - 2, tm), :]
    a = a + cw[1:2] * abuf[pl.ds(halo - 1, tm), :]
    a = a + cw[2:3] * abuf[pl.ds(halo, tm), :]
    gelu = 0.5 * a * (1.0 + lax.erf(a * math.sqrt(0.5)))
    o_ref[...] = (gelu * val).astype(o_ref.dtype)


def _ffn_up(x, g, w_up, conv_w, conv_b, *, seq):
    t, d = x.shape
    f = conv_w.shape[1]
    tm = min(1024, seq)
    tn = min(512, f)
    halo = 16
    rows = min(256, tm)
    nj = f // tn
    return pl.pallas_call(
        functools.partial(_ffn_up_kernel, rows=rows, halo=halo, tiles_per_seq=seq // tm),
        grid=(t // tm, nj),
        in_specs=[pl.BlockSpec((tm, d), lambda i, j: (i, 0)),
                  pl.BlockSpec((halo, d), lambda i, j: (jnp.maximum(i * (tm // halo) - 1, 0), 0)),
                  pl.BlockSpec((1, d), lambda i, j: (0, 0)),
                  pl.BlockSpec((d, tn), lambda i, j: (0, j)),
                  pl.BlockSpec((d, tn), lambda i, j: (0, nj + j)),
                  pl.BlockSpec((CONV_W, tn), lambda i, j: (0, j)),
                  pl.BlockSpec((1, tn), lambda i, j: (0, j))],
        out_specs=pl.BlockSpec((tm, tn), lambda i, j: (i, j)),
        out_shape=jax.ShapeDtypeStruct((t, f), BF16),
        scratch_shapes=[pltpu.VMEM((tm + halo, d), BF16), pltpu.VMEM((tm + halo, tn), F32)],
        compiler_params=_cparams(("parallel", "arbitrary")),
        name="ffn_up",
    )(x, x, g, w_up, w_up, conv_w, conv_b)


def _ffn_down_kernel(a_ref, w_ref, x_ref, gf_ref, o_ref, *, final):
    y = x_ref[...] + jnp.dot(a_ref[...], w_ref[...], preferred_element_type=F32)
    if final:
        y = _rmsnorm_rows(y, gf_ref[...])
    o_ref[...] = y


def _ffn_down(a, w, x, gf, *, final):
    t, d = x.shape
    f = a.shape[1]
    tm = min(256, t)
    return pl.pallas_call(
        functools.partial(_ffn_down_kernel, final=final),
        grid=(t // tm,),
        in_specs=[pl.BlockSpec((tm, f), lambda i: (i, 0)),
                  pl.BlockSpec((f, d), lambda i: (0, 0), pipeline_mode=pl.Buffered(1)),
                  pl.BlockSpec((tm, d), lambda i: (i, 0)),
                  pl.BlockSpec((1, d), lambda i: (0, 0))],
        out_specs=pl.BlockSpec((tm, d), lambda i: (i, 0)),
        out_shape=jax.ShapeDtypeStruct((t, d), F32),
        compiler_params=_cparams(("parallel",)),
        name="ffn_down",
    )(a, w, x, gf)


def _pack_w_in(w, d):
    o = 0
    parts = {}
    for name, width in (("aq", 512), ("ak", 512), ("av", 512), ("cq", 512), ("ckv", 256), ("kr", 64),
                        ("sq", 512), ("sk", 128), ("sv", 128), ("mq", 512), ("mk", 512), ("mv", 512),
                        ("zg", N_BRANCH * d)):
        parts[name] = w[:, o:o + width]
        o += width
    sq = parts["sq"].reshape(d, SWA_KV_HEADS, SWA_GROUP, SWA_DH).transpose(0, 2, 1, 3).reshape(d, 512)
    cols = [parts["zg"], parts["aq"], parts["ak"], parts["av"], parts["mq"], parts["mk"], parts["mv"],
            parts["cq"], parts["ckv"], parts["kr"], jnp.zeros((d, LANES - MLA_ROPE), w.dtype),
            sq, parts["sk"], parts["sv"], jnp.zeros((d, QKV_W - QKV_USED), w.dtype)]
    return jnp.concatenate(cols, axis=1).astype(BF16)


def _rope_tables(positions):
    half = MLA_ROPE // 2
    inv = ROPE_THETA ** (-jnp.arange(0, MLA_ROPE, 2, dtype=F32) / MLA_ROPE)
    ang = positions.astype(F32)[:, None] * inv[None, :]
    cos, sin = jnp.cos(ang), jnp.sin(ang)
    zh = jnp.zeros_like(cos)
    zr = jnp.zeros((positions.shape[0], LANES - MLA_ROPE), F32)
    cos_t = jnp.concatenate([cos, cos, zr], axis=1)
    sa_t = jnp.concatenate([-sin, zh, zr], axis=1)
    sb_t = jnp.concatenate([zh, sin, zr], axis=1)
    del half
    return cos_t, sa_t, sb_t


def kernel(x, positions, rel_bias, norm1_g, w_in, diff_lambda, diff_subln_g, mla_q_norm_g, mla_w_uq,
           mla_kv_norm_g, mla_w_ukv, swa_sinks, w_branch, w_out, norm2_g, ffn_w_up, ffn_conv_w,
           ffn_conv_b, ffn_w_down, final_norm_g):
    batch, seq, d = x.shape
    depth = w_in.shape[0]
    t = batch * seq
    gates_w = N_BRANCH * d
    assert gates_w % 1024 == 0 and seq % ATT_TQ == 0
    qkv_blk = gates_w // LANES

    tzd, tzc, sw = _bias_tables(rel_bias)
    sw = sw[BIAS_C0:BIAS_C0 + SWA_Q_HEADS]
    cos_t, sa_t, sb_t = _rope_tables(positions)

    xf = x.reshape(t, d)
    for l in range(depth):
        lam_init = 0.8 - 0.6 * math.exp(-0.3 * l)
        w_in_l = _pack_w_in(w_in[l], d)
        wuq = jnp.pad(mla_w_uq[l].reshape(MLA_Q_RANK, MLA_HEADS, MLA_NOPE + MLA_ROPE),
                      ((0, 0), (0, 0), (0, 2 * LANES - MLA_NOPE - MLA_ROPE))
                      ).reshape(MLA_Q_RANK, MLA_HEADS * 2 * LANES).astype(BF16)
        wukv = mla_w_ukv[l].astype(BF16)
        wb = w_branch[l]
        wb_swa = wb[2].reshape(SWA_KV_HEADS, SWA_GROUP, SWA_DH, d).transpose(1, 0, 2, 3).reshape(BRANCH_W, d)
        wb = jnp.stack([wb[0], wb[1], wb_swa, wb[3]]).astype(BF16)
        wo = w_out[l].astype(BF16)
        w_up = ffn_w_up[l].astype(BF16)
        w_down = ffn_w_down[l].astype(BF16)

        z = _in_proj(xf, norm1_g[l].reshape(1, d), w_in_l)
        br_a = _diff_attention(z, tzd, tzc, diff_lambda[l], diff_subln_g[l].reshape(1, DIFF_DV), lam_init,
                               batch=batch, seq=seq, qkv_col=gates_w)
        qm, km, vm = _mla_prep(z, mla_q_norm_g[l].reshape(1, MLA_Q_RANK), mla_kv_norm_g[l].reshape(1, MLA_KV_RANK),
                               wuq, wukv, cos_t, sa_t, sb_t, seq=seq, qkv_col=gates_w)
        br_b = _mla_attention(qm, km, vm, batch=batch, seq=seq)
        br_c = _swa_attention(z, sw, swa_sinks[l], batch=batch, seq=seq, qkv_blk=qkv_blk)
        br_d = _moba_attention(z, tzd, tzc, batch=batch, seq=seq, qkv_col=gates_w)
        xf = _merge(xf, z, (br_a, br_b, br_c, br_d), wb, wo)
        act = _ffn_up(xf, norm2_g[l].reshape(1, d), w_up, ffn_conv_w[l], ffn_conv_b[l].reshape(1, -1), seq=seq)
        xf = _ffn_down(act, w_down, xf, final_norm_g.reshape(1, d), final=(l == depth - 1))
    return xf.reshape(batch, seq, d)
```

```python
import functools
import math

import numpy as np
import jax
import jax.numpy as jnp
from jax import lax
from jax.experimental import pallas as pl
from jax.experimental.pallas import tpu as pltpu

F32 = jnp.float32
BF16 = jnp.bfloat16

DIFF_HEADS = 4
DIFF_DK = 64
DIFF_DV = 128
MLA_HEADS = 4
MLA_Q_RANK = 512
MLA_KV_RANK = 256
MLA_NOPE = 128
MLA_ROPE = 64
MLA_DV = 128
ROPE_THETA = 10000.0
SWA_Q_HEADS = 8
SWA_KV_HEADS = 2
SWA_GROUP = SWA_Q_HEADS // SWA_KV_HEADS
SWA_DH = 64
SWA_WINDOW = 128
MOBA_HEADS = 4
MOBA_DH = 128
MOBA_BLOCK = 256
MOBA_TOPK = 3
REL_BUCKETS = 32
REL_MAX_DIST = 128
BIAS_HEADS = DIFF_HEADS + SWA_Q_HEADS + MOBA_HEADS
BIAS_C0 = DIFF_HEADS
BIAS_D0 = DIFF_HEADS + SWA_Q_HEADS
N_BRANCH = 4
BRANCH_W = 512
CONV_W = 3
EPS = 1e-6
NEG_INF = -1e30

LANES = 128
ATT_TQ = 512
BIAS_CORNER = 128
SOFTMAX_ROWS = 64
CONV_PAD = 8
VMEM_LIMIT = 56 * 1024 * 1024

QKV_W = 5120
OFF_AQ, OFF_AK, OFF_AV = 0, 512, 1024
OFF_MQ, OFF_MK, OFF_MV = 1536, 2048, 2560
OFF_CQ, OFF_CKV, OFF_KR = 3072, 3584, 3840
OFF_SQ, OFF_SK, OFF_SV = 3968, 4480, 4608
QKV_USED = 4736

_NT = (((1,), (1,)), ((), ()))


def _t5_thresholds():
    n = np.arange(0, 2 * REL_MAX_DIST)
    max_exact = REL_BUCKETS // 2
    nf = np.maximum(n, 1).astype(np.float32)
    large = max_exact + (np.log(nf / np.float32(max_exact)) / np.float32(math.log(REL_MAX_DIST / max_exact))
                         * np.float32(REL_BUCKETS - max_exact)).astype(np.int32)
    bkt = np.where(n < max_exact, n, np.minimum(large, REL_BUCKETS - 1))
    return tuple(int(np.argmax(bkt >= b)) for b in range(1, REL_BUCKETS))


T5_THR = _t5_thresholds()


def _cparams(sem):
    return pltpu.CompilerParams(dimension_semantics=sem, vmem_limit_bytes=VMEM_LIMIT)


def _rmsnorm_rows(x, g):
    ms = jnp.mean(x * x, axis=-1, keepdims=True)
    return x * lax.rsqrt(ms + EPS) * g


def _norm_cast_kernel(x_ref, g_ref, h_ref):
    h_ref[...] = _rmsnorm_rows(x_ref[...], g_ref[...]).astype(h_ref.dtype)


def _norm_cast(x, g):
    t, d = x.shape
    tm = min(256, t)
    return pl.pallas_call(
        _norm_cast_kernel,
        grid=(t // tm,),
        in_specs=[pl.BlockSpec((tm, d), lambda i: (i, 0)), pl.BlockSpec((1, d), lambda i: (0, 0))],
        out_specs=pl.BlockSpec((tm, d), lambda i: (i, 0)),
        out_shape=jax.ShapeDtypeStruct((t, d), BF16),
        compiler_params=_cparams(("parallel",)),
        name="norm_cast",
    )(x, g)


def _in_proj_kernel(h_ref, w_ref, z_ref):
    z_ref[...] = jnp.dot(h_ref[...], w_ref[...], preferred_element_type=F32).astype(z_ref.dtype)


def _in_proj(h, w):
    t, d = h.shape
    nz = w.shape[1]
    tm = min(2048, t)
    tn = 1024
    return pl.pallas_call(
        _in_proj_kernel,
        grid=(t // tm, nz // tn),
        in_specs=[pl.BlockSpec((tm, d), lambda i, j: (i, 0)),
                  pl.BlockSpec((d, tn), lambda i, j: (0, j))],
        out_specs=pl.BlockSpec((tm, tn), lambda i, j: (i, j)),
        out_shape=jax.ShapeDtypeStruct((t, nz), BF16),
        compiler_params=_cparams(("parallel", "arbitrary")),
        name="in_proj",
    )(h, w)


def _bias_table_kernel(tab_ref, tzd_ref, tzc_ref, sw_ref, *, tq):
    h = pl.program_id(0)

    def lookup(n):
        val = jnp.full(n.shape, tab_ref[0, h], F32)
        for b in range(1, REL_BUCKETS):
            val = jnp.where(n >= T5_THR[b - 1], tab_ref[b, h], val)
        return val

    far = tab_ref[REL_BUCKETS - 1, h]
    r = lax.broadcasted_iota(jnp.int32, (tq, tq), 0)
    c = lax.broadcasted_iota(jnp.int32, (tq, tq), 1)
    rel = c - r
    tzd_ref[0] = jnp.where(rel >= 0, lookup(jnp.maximum(rel, 0)) - far, NEG_INF)
    r = lax.broadcasted_iota(jnp.int32, (BIAS_CORNER, BIAS_CORNER), 0)
    c = lax.broadcasted_iota(jnp.int32, (BIAS_CORNER, BIAS_CORNER), 1)
    tzc_ref[0] = lookup(BIAS_CORNER + c - r) - far
    r = lax.broadcasted_iota(jnp.int32, (2 * SWA_WINDOW, SWA_WINDOW), 0)
    c = lax.broadcasted_iota(jnp.int32, (2 * SWA_WINDOW, SWA_WINDOW), 1)
    rel = c - r + SWA_WINDOW
    valid = (rel >= 0) & (rel < SWA_WINDOW)
    sw_ref[0] = jnp.where(valid, lookup(jnp.maximum(rel, 0)), NEG_INF)


def _bias_tables(rel_bias):
    tq = ATT_TQ
    return pl.pallas_call(
        functools.partial(_bias_table_kernel, tq=tq),
        grid=(BIAS_HEADS,),
        in_specs=[pl.BlockSpec(memory_space=pltpu.SMEM)],
        out_specs=[pl.BlockSpec((1, tq, tq), lambda h: (h, 0, 0)),
                   pl.BlockSpec((1, BIAS_CORNER, BIAS_CORNER), lambda h: (h, 0, 0)),
                   pl.BlockSpec((1, 2 * SWA_WINDOW, SWA_WINDOW), lambda h: (h, 0, 0))],
        out_shape=[jax.ShapeDtypeStruct((BIAS_HEADS, tq, tq), F32),
                   jax.ShapeDtypeStruct((BIAS_HEADS, BIAS_CORNER, BIAS_CORNER), F32),
                   jax.ShapeDtypeStruct((BIAS_HEADS, 2 * SWA_WINDOW, SWA_WINDOW), F32)],
        compiler_params=_cparams(("arbitrary",)),
        name="bias_tables",
    )(rel_bias)


def _fold_rows(x, op):
    rows = x.shape[0]
    while rows > 8 and rows % 2 == 0:
        rows //= 2
        x = op(x[:rows], x[rows:])
    return x


def _softmax_pv(s_ref, p_ref, vt, m_ref, l_ref, acc_ref, first):
    tk = s_ref.shape[0]
    ch = SOFTMAX_ROWS
    m8 = None
    for r0 in range(0, tk, ch):
        x = _fold_rows(s_ref[r0:r0 + ch, :], jnp.maximum)
        m8 = x if m8 is None else jnp.maximum(m8, x)
    m_new = jnp.max(m8, axis=0, keepdims=True)
    if not first:
        m_old = m_ref[...]
        m_new = jnp.maximum(m_old, m_new)
        alpha = jnp.exp(m_old - m_new)
    s8 = None
    for r0 in range(0, tk, ch):
        p = jnp.exp(s_ref[r0:r0 + ch, :] - m_new)
        p_ref[r0:r0 + ch, :] = p.astype(p_ref.dtype)
        x = _fold_rows(p, jnp.add)
        s8 = x if s8 is None else s8 + x
    lsum = jnp.sum(s8, axis=0, keepdims=True)
    pv = jnp.dot(vt, p_ref[...], preferred_element_type=F32)
    m_ref[...] = m_new
    if first:
        l_ref[...] = lsum
        acc_ref[...] = pv
    else:
        l_ref[...] = alpha * l_ref[...] + lsum
        acc_ref[...] = alpha * acc_ref[...] + pv


def _causal_sweep(qi, scores, softmax_pv, s0, s1):
    scores(qi, s0, "diag")
    scores(qi - 1, s1, "prev")
    softmax_pv(qi, s0, True)

    @pl.when(qi >= 1)
    def _():
        scores(qi - 2, s0, "far")
        softmax_pv(qi - 1, s1, False)

    def body(p, carry):
        j = qi - 2 - 2 * p
        scores(j - 1, s1, "far")
        softmax_pv(j, s0, False)

        @pl.when(j >= 1)
        def _():
            scores(j - 2, s0, "far")
            softmax_pv(j - 1, s1, False)
        return carry

    lax.fori_loop(0, qi // 2, body, 0)


def _transpose_values(v_ref, vt_ref, tk):
    def body(c, carry):
        start = pl.multiple_of(c * tk, tk)
        vt_ref[c] = v_ref[pl.ds(start, tk), :].astype(F32).T.astype(vt_ref.dtype)
        return carry
    lax.fori_loop(0, vt_ref.shape[0], body, 0)


def _k_tile(k_ref, j, tk):
    return k_ref[pl.ds(pl.multiple_of(jnp.maximum(j, 0) * tk, tk), tk), :]


def _add_corner(st, corner):
    n = corner.shape[0]
    tk = st.shape[0]
    bottom = st[tk - n:]
    bottom = jnp.concatenate([bottom[:, :n] + corner, bottom[:, n:]], axis=1)
    return jnp.concatenate([st[:tk - n], bottom], axis=0)


def _diff_attn_kernel(q_ref, k_ref, v_ref, tzd_ref, tzc_ref, lam_ref, g_ref, o_ref,
                      vt_ref, qm_ref, s0, s1, p_ref, m, l, acc, *, tq, lam_init):
    qi = pl.program_id(1)
    chains = 2 * DIFF_HEADS

    @pl.when(qi == 0)
    def _():
        _transpose_values(v_ref, vt_ref, tq)

    lane = lax.broadcasted_iota(jnp.int32, (tq, LANES), 1)
    for h in range(DIFF_HEADS):
        qs = q_ref[:, h * LANES:(h + 1) * LANES] * (DIFF_DK ** -0.5)
        zero = jnp.zeros_like(qs)
        qm_ref[2 * h] = jnp.where(lane < DIFF_DK, qs, zero)
        qm_ref[2 * h + 1] = jnp.where(lane >= DIFF_DK, qs, zero)

    def scores(j, sbuf, kind):
        kt = _k_tile(k_ref, j, tq)
        for c in range(chains):
            h = c // 2
            st = lax.dot_general(kt[:, h * LANES:(h + 1) * LANES], qm_ref[c], _NT, preferred_element_type=F32)
            if kind == "diag":
                st = st + tzd_ref[h]
            elif kind == "prev":
                st = _add_corner(st, tzc_ref[h])
            sbuf[c] = st

    def softmax_pv(j, sbuf, first):
        vt = vt_ref[j]
        for c in range(chains):
            h = c // 2
            _softmax_pv(sbuf.at[c], p_ref.at[c], vt[h * DIFF_DV:(h + 1) * DIFF_DV, :],
                        m.at[c], l.at[c], acc.at[c], first)

    _causal_sweep(qi, scores, softmax_pv, s0, s1)

    lv = lam_ref[...]
    e1 = jnp.exp(jnp.sum(lv[0:1] * lv[1:2], axis=-1, keepdims=True))
    e2 = jnp.exp(jnp.sum(lv[2:3] * lv[3:4], axis=-1, keepdims=True))
    lam = e1 - e2 + lam_init
    for h in range(DIFF_HEADS):
        ot = acc[2 * h] / l[2 * h] - lam * (acc[2 * h + 1] / l[2 * h + 1])
        ot = ot * lax.rsqrt(jnp.mean(ot * ot, axis=0, keepdims=True) + EPS)
        o_ref[:, h * DIFF_DV:(h + 1) * DIFF_DV] = (ot.T * g_ref[...] * (1.0 - lam_init)).astype(o_ref.dtype)


def _diff_attention(z, tzd, tzc, lam_vecs, subln_g, lam_init, *, batch, seq, qkv_col):
    tq = ATT_TQ
    nq = seq // tq
    hw = DIFF_HEADS * LANES
    qb, kb, vb = ((qkv_col + off) // hw for off in (OFF_AQ, OFF_AK, OFF_AV))
    chains = 2 * DIFF_HEADS
    return pl.pallas_call(
        functools.partial(_diff_attn_kernel, tq=tq, lam_init=lam_init),
        grid=(batch, nq),
        in_specs=[pl.BlockSpec((tq, hw), lambda b, i: (b * nq + i, qb)),
                  pl.BlockSpec((seq, hw), lambda b, i: (b, kb), pipeline_mode=pl.Buffered(1)),
                  pl.BlockSpec((seq, hw), lambda b, i: (b, vb), pipeline_mode=pl.Buffered(1)),
                  pl.BlockSpec((DIFF_HEADS, tq, tq), lambda b, i: (0, 0, 0), pipeline_mode=pl.Buffered(1)),
                  pl.BlockSpec((DIFF_HEADS, BIAS_CORNER, BIAS_CORNER), lambda b, i: (0, 0, 0)),
                  pl.BlockSpec((4, DIFF_DK), lambda b, i: (0, 0)),
                  pl.BlockSpec((1, DIFF_DV), lambda b, i: (0, 0))],
        out_specs=pl.BlockSpec((tq, DIFF_HEADS * DIFF_DV), lambda b, i: (b * nq + i, 0)),
        out_shape=jax.ShapeDtypeStruct((batch * seq, DIFF_HEADS * DIFF_DV), BF16),
        scratch_shapes=[pltpu.VMEM((nq, DIFF_HEADS * DIFF_DV, tq), BF16),
                        pltpu.VMEM((chains, tq, LANES), BF16),
                        pltpu.VMEM((chains, tq, tq), F32), pltpu.VMEM((chains, tq, tq), F32),
                        pltpu.VMEM((chains, tq, tq), BF16),
                        pltpu.VMEM((chains, 1, tq), F32), pltpu.VMEM((chains, 1, tq), F32),
                        pltpu.VMEM((chains, DIFF_DV, tq), F32)],
        compiler_params=_cparams(("parallel", "arbitrary")),
        name="diff_attn",
    )(z, z, z, tzd, tzc, lam_vecs, subln_g)


def _mla_prep_kernel(cq_ref, ckv_ref, kr_ref, gq_ref, gkv_ref, wuq_ref, wukv_ref,
                     cos_ref, sa_ref, sb_ref, qm_ref, km_ref, vm_ref):
    def rope(x):
        return (x * cos_ref[...] + pltpu.roll(x, 96, 1) * sa_ref[...]
                + pltpu.roll(x, 32, 1) * sb_ref[...])

    cq = _rmsnorm_rows(cq_ref[...].astype(F32), gq_ref[...]).astype(BF16)
    q = jnp.dot(cq, wuq_ref[...], preferred_element_type=F32)
    ckv = _rmsnorm_rows(ckv_ref[...].astype(F32), gkv_ref[...]).astype(BF16)
    kv = jnp.dot(ckv, wukv_ref[...], preferred_element_type=F32)
    krr = rope(kr_ref[...].astype(F32)).astype(BF16)
    scale = (MLA_NOPE + MLA_ROPE) ** -0.5
    hw = 2 * LANES
    for h in range(MLA_HEADS):
        qm_ref[:, h * hw:h * hw + LANES] = (q[:, h * hw:h * hw + LANES] * scale).astype(BF16)
        qm_ref[:, h * hw + LANES:(h + 1) * hw] = (rope(q[:, h * hw + LANES:(h + 1) * hw]) * scale).astype(BF16)
        km_ref[:, h * hw:h * hw + LANES] = kv[:, h * hw:h * hw + LANES].astype(BF16)
        km_ref[:, h * hw + LANES:(h + 1) * hw] = krr
        vm_ref[:, h * MLA_DV:(h + 1) * MLA_DV] = kv[:, h * hw + LANES:(h + 1) * hw].astype(BF16)


def _mla_prep(z, gq, gkv, wuq, wukv, cos_t, sa_t, sb_t, *, seq, qkv_col):
    t = z.shape[0]
    tm = min(512, seq)
    ns = seq // tm
    hw = 2 * LANES
    cq_blk = (qkv_col + OFF_CQ) // MLA_Q_RANK
    ckv_blk = (qkv_col + OFF_CKV) // MLA_KV_RANK
    kr_blk = (qkv_col + OFF_KR) // LANES
    rope_spec = pl.BlockSpec((tm, LANES), lambda i: (i % ns, 0))
    return pl.pallas_call(
        _mla_prep_kernel,
        grid=(t // tm,),
        in_specs=[pl.BlockSpec((tm, MLA_Q_RANK), lambda i: (i, cq_blk)),
                  pl.BlockSpec((tm, MLA_KV_RANK), lambda i: (i, ckv_blk)),
                  pl.BlockSpec((tm, LANES), lambda i: (i, kr_blk)),
                  pl.BlockSpec((1, MLA_Q_RANK), lambda i: (0, 0)),
                  pl.BlockSpec((1, MLA_KV_RANK), lambda i: (0, 0)),
                  pl.BlockSpec((MLA_Q_RANK, MLA_HEADS * hw), lambda i: (0, 0)),
                  pl.BlockSpec((MLA_KV_RANK, MLA_HEADS * hw), lambda i: (0, 0)),
                  rope_spec, rope_spec, rope_spec],
        out_specs=[pl.BlockSpec((tm, MLA_HEADS * hw), lambda i: (i, 0)),
                   pl.BlockSpec((tm, MLA_HEADS * hw), lambda i: (i, 0)),
                   pl.BlockSpec((tm, MLA_HEADS * MLA_DV), lambda i: (i, 0))],
        out_shape=[jax.ShapeDtypeStruct((t, MLA_HEADS * hw), BF16),
                   jax.ShapeDtypeStruct((t, MLA_HEADS * hw), BF16),
                   jax.ShapeDtypeStruct((t, MLA_HEADS * MLA_DV), BF16)],
        compiler_params=_cparams(("parallel",)),
        name="mla_prep",
    )(z, z, z, gq, gkv, wuq, wukv, cos_t, sa_t, sb_t)


def _mla_attn_kernel(q_ref, k_ref, v_ref, o_ref, vt_ref, s0, s1, p_ref, m, l, acc, *, tq):
    qi = pl.program_id(1)
    hw = 2 * LANES

    @pl.when(qi == 0)
    def _():
        _transpose_values(v_ref, vt_ref, tq)

    def scores(j, sbuf, kind):
        kt = _k_tile(k_ref, j, tq)
        for h in range(MLA_HEADS):
            st = lax.dot_general(kt[:, h * hw:(h + 1) * hw], q_ref[:, h * hw:(h + 1) * hw], _NT,
                                 preferred_element_type=F32)
            if kind == "diag":
                r = lax.broadcasted_iota(jnp.int32, st.shape, 0)
                c = lax.broadcasted_iota(jnp.int32, st.shape, 1)
                st = jnp.where(r <= c, st, NEG_INF)
            sbuf[h] = st

    def softmax_pv(j, sbuf, first):
        vt = vt_ref[j]
        for h in range(MLA_HEADS):
            _softmax_pv(sbuf.at[h], p_ref.at[h], vt[h * MLA_DV:(h + 1) * MLA_DV, :],
                        m.at[h], l.at[h], acc.at[h], first)

    _causal_sweep(qi, scores, softmax_pv, s0, s1)
    for h in range(MLA_HEADS):
        o_ref[:, h * MLA_DV:(h + 1) * MLA_DV] = (acc[h] / l[h]).T.astype(o_ref.dtype)


def _mla_attention(qm, km, vm, *, batch, seq):
    tq = ATT_TQ
    nq = seq // tq
    hw = 2 * LANES
    return pl.pallas_call(
        functools.partial(_mla_attn_kernel, tq=tq),
        grid=(batch, nq),
        in_specs=[pl.BlockSpec((tq, MLA_HEADS * hw), lambda b, i: (b * nq + i, 0)),
                  pl.BlockSpec((seq, MLA_HEADS * hw), lambda b, i: (b, 0)),
                  pl.BlockSpec((seq, MLA_HEADS * MLA_DV), lambda b, i: (b, 0))],
        out_specs=pl.BlockSpec((tq, MLA_HEADS * MLA_DV), lambda b, i: (b * nq + i, 0)),
        out_shape=jax.ShapeDtypeStruct((batch * seq, MLA_HEADS * MLA_DV), BF16),
        scratch_shapes=[pltpu.VMEM((nq, MLA_HEADS * MLA_DV, tq), BF16),
                        pltpu.VMEM((MLA_HEADS, tq, tq), F32), pltpu.VMEM((MLA_HEADS, tq, tq), F32),
                        pltpu.VMEM((MLA_HEADS, tq, tq), BF16),
                        pltpu.VMEM((MLA_HEADS, 1, tq), F32), pltpu.VMEM((MLA_HEADS, 1, tq), F32),
                        pltpu.VMEM((MLA_HEADS, MLA_DV, tq), F32)],
        compiler_params=_cparams(("parallel", "arbitrary")),
        name="mla_attn",
    )(qm, km, vm)


def _swa_kernel(q0, q1, q2, q3, kc_ref, kp_ref, vc_ref, vp_ref, sw_ref, sink_ref, o_ref, *, nsub):
    n = pl.program_id(1)
    w = SWA_WINDOW
    kcat = jnp.concatenate([kp_ref[...], kc_ref[...]], axis=0)
    vt = jnp.concatenate([vp_ref[...], vc_ref[...]], axis=0).astype(F32).T.astype(BF16)
    lane = lax.broadcasted_iota(jnp.int32, (w, LANES), 1)
    row = lax.broadcasted_iota(jnp.int32, (2 * w, w), 0)
    orow = lax.broadcasted_iota(jnp.int32, (LANES, w), 0)
    first = jnp.where((row < w) & (n == 0), NEG_INF, 0.0)
    for i in range(nsub):
        k2 = kcat[i * w:(i + 2) * w]
        v2t = vt[:, i * w:(i + 2) * w]
        for g, q_ref in enumerate((q0, q1, q2, q3)):
            qs = q_ref[i * w:(i + 1) * w, :] * (SWA_DH ** -0.5)
            zero = jnp.zeros_like(qs)
            outs = []
            for kv in range(SWA_KV_HEADS):
                head = kv * SWA_GROUP + g
                qm = jnp.where((lane >= kv * SWA_DH) & (lane < (kv + 1) * SWA_DH), qs, zero)
                st = lax.dot_general(k2, qm, _NT, preferred_element_type=F32) + sw_ref[head]
                if i == 0:
                    st = st + first
                sink = sink_ref[head]
                m = jnp.maximum(jnp.max(_fold_rows(st, jnp.maximum), axis=0, keepdims=True), sink)
                p = jnp.exp(st - m)
                denom = jnp.sum(_fold_rows(p, jnp.add), axis=0, keepdims=True) + jnp.exp(sink - m)
                ot = jnp.dot(v2t, p.astype(BF16), preferred_element_type=F32)
                outs.append(ot / denom)
            og = jnp.where(orow < SWA_DH, outs[0], outs[1]).T
            o_ref[i * w:(i + 1) * w, g * LANES:(g + 1) * LANES] = og.astype(o_ref.dtype)


def _swa_attention(z, sw, sinks, *, batch, seq, qkv_blk):
    w = SWA_WINDOW
    nsub = 4
    tq = nsub * w
    nb = seq // tq
    qb = qkv_blk + OFF_SQ // LANES
    kb = qkv_blk + OFF_SK // LANES
    vb = qkv_blk + OFF_SV // LANES

    def cur(col):
        return pl.BlockSpec((tq, LANES), lambda b, n: (b * nb + n, col))

    def prev(col):
        return pl.BlockSpec((w, LANES), lambda b, n: ((b * nb + n) * nsub - jnp.minimum(n, 1), col))

    return pl.pallas_call(
        functools.partial(_swa_kernel, nsub=nsub),
        grid=(batch, nb),
        in_specs=[cur(qb), cur(qb + 1), cur(qb + 2), cur(qb + 3),
                  cur(kb), prev(kb), cur(vb), prev(vb),
                  pl.BlockSpec((SWA_Q_HEADS, 2 * w, w), lambda b, n: (0, 0, 0)),
                  pl.BlockSpec(memory_space=pltpu.SMEM)],
        out_specs=pl.BlockSpec((tq, SWA_Q_HEADS * SWA_DH), lambda b, n: (b * nb + n, 0)),
        out_shape=jax.ShapeDtypeStruct((batch * seq, SWA_Q_HEADS * SWA_DH), BF16),
        compiler_params=_cparams(("parallel", "arbitrary")),
        name="swa_attn",
    )(z, z, z, z, z, z, z, z, sw, sinks)


def _moba_kernel(q_ref, k_ref, v_ref, tzd_ref, tzc_ref, o_ref, vt_ref, kmean, selm, s0, s1, p_ref,
                 m, l, acc, *, tq, nblk):
    qi = pl.program_id(1)
    dh = MOBA_DH
    blk = MOBA_BLOCK

    @pl.when(qi == 0)
    def _():
        _transpose_values(v_ref, vt_ref, tq)
        kmean[...] = jnp.zeros_like(kmean)
        for n in range(nblk):
            km = jnp.mean(k_ref[n * blk:(n + 1) * blk, :].astype(F32), axis=0, keepdims=True)
            for h in range(MOBA_HEADS):
                kmean[h, n:n + 1, :] = km[:, h * dh:(h + 1) * dh]

    in_b = lax.broadcasted_iota(jnp.int32, (1, tq), 1) >= blk
    own = 2 * qi + jnp.where(in_b, 1, 0)
    for h in range(MOBA_HEADS):
        q = q_ref[:, h * dh:(h + 1) * dh]
        km = kmean[h]
        km_hi = km.astype(BF16)
        km_lo = (km - km_hi.astype(F32)).astype(BF16)
        gate = (lax.dot_general(km_hi, q, _NT, preferred_element_type=F32)
                + lax.dot_general(km_lo, q, _NT, preferred_element_type=F32))
        row = lax.broadcasted_iota(jnp.int32, gate.shape, 0)
        valid = row < own
        g = jnp.where(valid, gate, NEG_INF)
        rank = jnp.zeros(gate.shape, jnp.int32)
        for n in range(nblk):
            other = g[n:n + 1, :]
            ahead = (other > g) | ((other == g) & (row > n))
            rank = rank + jnp.where(ahead, 1, 0)
        selm[h] = jnp.where(valid & (rank < MOBA_TOPK), 0.0, NEG_INF)

    scale = dh ** -0.5

    def scores(j, sbuf, kind):
        kt = _k_tile(k_ref, j, tq)
        jc = jnp.maximum(j, 0)
        for h in range(MOBA_HEADS):
            st = lax.dot_general(kt[:, h * dh:(h + 1) * dh], q_ref[:, h * dh:(h + 1) * dh], _NT,
                                 preferred_element_type=F32) * scale
            sel_a = selm[h, pl.ds(2 * jc, 1), :]
            if kind == "diag":
                st = st + tzd_ref[h]
                st = jnp.concatenate([st[:blk] + jnp.where(in_b, sel_a, 0.0), st[blk:]], axis=0)
            else:
                if kind == "prev":
                    st = _add_corner(st, tzc_ref[h])
                st = jnp.concatenate([st[:blk] + sel_a, st[blk:] + selm[h, pl.ds(2 * jc + 1, 1), :]], axis=0)
            sbuf[h] = st

    def softmax_pv(j, sbuf, first):
        vt = vt_ref[j]
        for h in range(MOBA_HEADS):
            _softmax_pv(sbuf.at[h], p_ref.at[h], vt[h * dh:(h + 1) * dh, :], m.at[h], l.at[h], acc.at[h], first)

    _causal_sweep(qi, scores, softmax_pv, s0, s1)
    for h in range(MOBA_HEADS):
        o_ref[:, h * dh:(h + 1) * dh] = (acc[h] / l[h]).T.astype(o_ref.dtype)


def _moba_attention(z, tzd, tzc, *, batch, seq, qkv_col):
    tq = ATT_TQ
    assert tq == 2 * MOBA_BLOCK and seq % tq == 0
    nq = seq // tq
    nblk = seq // MOBA_BLOCK
    nrow = -(-nblk // 8) * 8
    hw = MOBA_HEADS * MOBA_DH
    qb, kb, vb = ((qkv_col + off) // hw for off in (OFF_MQ, OFF_MK, OFF_MV))
    hb = BIAS_D0 // MOBA_HEADS
    return pl.pallas_call(
        functools.partial(_moba_kernel, tq=tq, nblk=nblk),
        grid=(batch, nq),
        in_specs=[pl.BlockSpec((tq, hw), lambda b, i: (b * nq + i, qb)),
                  pl.BlockSpec((seq, hw), lambda b, i: (b, kb)),
                  pl.BlockSpec((seq, hw), lambda b, i: (b, vb)),
                  pl.BlockSpec((MOBA_HEADS, tq, tq), lambda b, i: (hb, 0, 0)),
                  pl.BlockSpec((MOBA_HEADS, BIAS_CORNER, BIAS_CORNER), lambda b, i: (hb, 0, 0))],
        out_specs=pl.BlockSpec((tq, hw), lambda b, i: (b * nq + i, 0)),
        out_shape=jax.ShapeDtypeStruct((batch * seq, hw), BF16),
        scratch_shapes=[pltpu.VMEM((nq, hw, tq), BF16), pltpu.VMEM((MOBA_HEADS, nrow, MOBA_DH), F32),
                        pltpu.VMEM((MOBA_HEADS, nrow, tq), F32),
                        pltpu.VMEM((MOBA_HEADS, tq, tq), F32), pltpu.VMEM((MOBA_HEADS, tq, tq), F32),
                        pltpu.VMEM((MOBA_HEADS, tq, tq), BF16),
                        pltpu.VMEM((MOBA_HEADS, 1, tq), F32), pltpu.VMEM((MOBA_HEADS, 1, tq), F32),
                        pltpu.VMEM((MOBA_HEADS, MOBA_DH, tq), F32)],
        compiler_params=_cparams(("parallel", "arbitrary")),
        name="moba_attn",
    )(z, z, z, tzd, tzc)


def _merge_kernel(x_ref, a_ref, b_ref, c_ref, d_ref, g0, g1, g2, g3, wb_ref, wo_ref, gn_ref, o_ref, h_ref):
    merged = None
    for i, (br, gr) in enumerate(((a_ref, g0), (b_ref, g1), (c_ref, g2), (d_ref, g3))):
        y = jnp.dot(br[...], wb_ref[i], preferred_element_type=F32)
        gate = 1.0 / (1.0 + jnp.exp(-gr[...].astype(F32)))
        merged = gate * y if merged is None else merged + gate * y
    x_new = x_ref[...] + jnp.dot(merged.astype(BF16), wo_ref[...], preferred_element_type=F32)
    o_ref[...] = x_new
    h_ref[...] = _rmsnorm_rows(x_new, gn_ref[...]).astype(h_ref.dtype)


def _merge(x, z, branches, wb, wo, g_next):
    t, d = x.shape
    tm = min(256, t)
    row = lambda i: (i, 0)
    const1 = pl.Buffered(1)
    return pl.pallas_call(
        _merge_kernel,
        grid=(t // tm,),
        in_specs=[pl.BlockSpec((tm, d), row)]
                 + [pl.BlockSpec((tm, BRANCH_W), row)] * N_BRANCH
                 + [pl.BlockSpec((tm, d), functools.partial(lambda i, k: (i, k), k=k)) for k in range(N_BRANCH)]
                 + [pl.BlockSpec((N_BRANCH, BRANCH_W, d), lambda i: (0, 0, 0), pipeline_mode=const1),
                    pl.BlockSpec((d, d), lambda i: (0, 0), pipeline_mode=const1),
                    pl.BlockSpec((1, d), lambda i: (0, 0))],
        out_specs=[pl.BlockSpec((tm, d), row), pl.BlockSpec((tm, d), row)],
        out_shape=[jax.ShapeDtypeStruct((t, d), F32), jax.ShapeDtypeStruct((t, d), BF16)],
        compiler_params=_cparams(("parallel",)),
        name="merge_out",
    )(x, *branches, z, z, z, z, wb, wo, g_next)


def _ffn_up_kernel(h_ref, wa_ref, wv_ref, cw_ref, cb_ref, o_ref, abuf, tail, *, tiles_per_seq):
    i = pl.program_id(0)
    j = pl.program_id(1)
    tm = h_ref.shape[0]
    pad = CONV_PAD

    @pl.when(i == 0)
    def _():
        tail[j] = jnp.zeros(tail.shape[1:], F32)

    seq_start = i % tiles_per_seq == 0
    abuf[0:pad, :] = jnp.where(seq_start, 0.0, tail[j])
    abuf[pad:, :] = jnp.dot(h_ref[...], wa_ref[...], preferred_element_type=F32)
    val = jnp.dot(h_ref[...], wv_ref[...], preferred_element_type=F32)
    tail[j] = abuf[tm:, :]
    cw = cw_ref[...]
    c = cb_ref[...] + cw[0:1] * abuf[pl.ds(pad - 2, tm), :]
    c = c + cw[1:2] * abuf[pl.ds(pad - 1, tm), :]
    c = c + cw[2:3] * abuf[pl.ds(pad, tm), :]
    gelu = 0.5 * c * (1.0 + lax.erf(c * math.sqrt(0.5)))
    o_ref[...] = (gelu * val).astype(o_ref.dtype)


def _ffn_up(h, w_up, conv_w, conv_b, *, seq):
    t, d = h.shape
    f = conv_w.shape[1]
    tm = min(1024, seq)
    tn = min(512, f)
    nj = f // tn
    return pl.pallas_call(
        functools.partial(_ffn_up_kernel, tiles_per_seq=seq // tm),
        grid=(t // tm, nj),
        in_specs=[pl.BlockSpec((tm, d), lambda i, j: (i, 0)),
                  pl.BlockSpec((d, tn), lambda i, j: (0, j)),
                  pl.BlockSpec((d, tn), lambda i, j: (0, nj + j)),
                  pl.BlockSpec((CONV_W, tn), lambda i, j: (0, j)),
                  pl.BlockSpec((1, tn), lambda i, j: (0, j))],
        out_specs=pl.BlockSpec((tm, tn), lambda i, j: (i, j)),
        out_shape=jax.ShapeDtypeStruct((t, f), BF16),
        scratch_shapes=[pltpu.VMEM((tm + CONV_PAD, tn), F32), pltpu.VMEM((nj, CONV_PAD, tn), F32)],
        compiler_params=_cparams(("arbitrary", "arbitrary")),
        name="ffn_up",
    )(h, w_up, w_up, conv_w, conv_b)


def _ffn_down_kernel(a_ref, w_ref, x_ref, gn_ref, *out_refs, final):
    y = x_ref[...] + jnp.dot(a_ref[...], w_ref[...], preferred_element_type=F32)
    if final:
        out_refs[0][...] = _rmsnorm_rows(y, gn_ref[...])
    else:
        out_refs[0][...] = y
        out_refs[1][...] = _rmsnorm_rows(y, gn_ref[...]).astype(out_refs[1].dtype)


def _ffn_down(a, w, x, g_next, *, final):
    t, d = x.shape
    f = a.shape[1]
    tm = min(256, t)
    row = pl.BlockSpec((tm, d), lambda i: (i, 0))
    out_specs = [row] if final else [row, row]
    out_shape = [jax.ShapeDtypeStruct((t, d), F32)] + ([] if final else [jax.ShapeDtypeStruct((t, d), BF16)])
    return pl.pallas_call(
        functools.partial(_ffn_down_kernel, final=final),
        grid=(t // tm,),
        in_specs=[pl.BlockSpec((tm, f), lambda i: (i, 0)),
                  pl.BlockSpec((f, d), lambda i: (0, 0), pipeline_mode=pl.Buffered(1)),
                  row,
                  pl.BlockSpec((1, d), lambda i: (0, 0))],
        out_specs=out_specs,
        out_shape=out_shape,
        compiler_params=_cparams(("parallel",)),
        name="ffn_down",
    )(a, w, x, g_next)


def _pack_w_in(w, d):
    o = 0
    parts = {}
    for name, width in (("aq", 512), ("ak", 512), ("av", 512), ("cq", 512), ("ckv", 256), ("kr", 64),
                        ("sq", 512), ("sk", 128), ("sv", 128), ("mq", 512), ("mk", 512), ("mv", 512),
                        ("zg", N_BRANCH * d)):
        parts[name] = w[:, o:o + width]
        o += width
    sq = parts["sq"].reshape(d, SWA_KV_HEADS, SWA_GROUP, SWA_DH).transpose(0, 2, 1, 3).reshape(d, 512)
    cols = [parts["zg"], parts["aq"], parts["ak"], parts["av"], parts["mq"], parts["mk"], parts["mv"],
            parts["cq"], parts["ckv"], parts["kr"], jnp.zeros((d, LANES - MLA_ROPE), w.dtype),
            sq, parts["sk"], parts["sv"], jnp.zeros((d, QKV_W - QKV_USED), w.dtype)]
    return jnp.concatenate(cols, axis=1).astype(BF16)


def _rope_tables(positions):
    half = MLA_ROPE // 2
    inv = ROPE_THETA ** (-jnp.arange(0, MLA_ROPE, 2, dtype=F32) / MLA_ROPE)
    ang = positions.astype(F32)[:, None] * inv[None, :]
    cos, sin = jnp.cos(ang), jnp.sin(ang)
    zh = jnp.zeros_like(cos)
    zr = jnp.zeros((positions.shape[0], LANES - MLA_ROPE), F32)
    cos_t = jnp.concatenate([cos, cos, zr], axis=1)
    sa_t = jnp.concatenate([-sin, zh, zr], axis=1)
    sb_t = jnp.concatenate([zh, sin, zr], axis=1)
    del half
    return cos_t, sa_t, sb_t


def kernel(x, positions, rel_bias, norm1_g, w_in, diff_lambda, diff_subln_g, mla_q_norm_g, mla_w_uq,
           mla_kv_norm_g, mla_w_ukv, swa_sinks, w_branch, w_out, norm2_g, ffn_w_up, ffn_conv_w,
           ffn_conv_b, ffn_w_down, final_norm_g):
    batch, seq, d = x.shape
    depth = w_in.shape[0]
    t = batch * seq
    gates_w = N_BRANCH * d
    assert gates_w % 1024 == 0 and seq % ATT_TQ == 0
    qkv_blk = gates_w // LANES

    tzd, tzc, sw = _bias_tables(rel_bias)
    sw = sw[BIAS_C0:BIAS_C0 + SWA_Q_HEADS]
    cos_t, sa_t, sb_t = _rope_tables(positions)

    xf = x.reshape(t, d)
    hn = _norm_cast(xf, norm1_g[0].reshape(1, d))
    for l in range(depth):
        lam_init = 0.8 - 0.6 * math.exp(-0.3 * l)
        w_in_l = _pack_w_in(w_in[l], d)
        wuq = jnp.pad(mla_w_uq[l].reshape(MLA_Q_RANK, MLA_HEADS, MLA_NOPE + MLA_ROPE),
                      ((0, 0), (0, 0), (0, 2 * LANES - MLA_NOPE - MLA_ROPE))
                      ).reshape(MLA_Q_RANK, MLA_HEADS * 2 * LANES).astype(BF16)
        wukv = mla_w_ukv[l].astype(BF16)
        wb = w_branch[l]
        wb_swa = wb[2].reshape(SWA_KV_HEADS, SWA_GROUP, SWA_DH, d).transpose(1, 0, 2, 3).reshape(BRANCH_W, d)
        wb = jnp.stack([wb[0], wb[1], wb_swa, wb[3]]).astype(BF16)
        wo = w_out[l].astype(BF16)
        w_up = ffn_w_up[l].astype(BF16)
        w_down = ffn_w_down[l].astype(BF16)

        z = _in_proj(hn, w_in_l)
        br_a = _diff_attention(z, tzd, tzc, diff_lambda[l], diff_subln_g[l].reshape(1, DIFF_DV), lam_init,
                               batch=batch, seq=seq, qkv_col=gates_w)
        qm, km, vm = _mla_prep(z, mla_q_norm_g[l].reshape(1, MLA_Q_RANK), mla_kv_norm_g[l].reshape(1, MLA_KV_RANK),
                               wuq, wukv, cos_t, sa_t, sb_t, seq=seq, qkv_col=gates_w)
        br_b = _mla_attention(qm, km, vm, batch=batch, seq=seq)
        br_c = _swa_attention(z, sw, swa_sinks[l], batch=batch, seq=seq, qkv_blk=qkv_blk)
        br_d = _moba_attention(z, tzd, tzc, batch=batch, seq=seq, qkv_col=gates_w)
        xf, h2 = _merge(xf, z, (br_a, br_b, br_c, br_d), wb, wo, norm2_g[l].reshape(1, d))
        act = _ffn_up(h2, w_up, ffn_conv_w[l], ffn_conv_b[l].reshape(1, -1), seq=seq)
        if l == depth - 1:
            (xf,) = _ffn_down(act, w_down, xf, final_norm_g.reshape(1, d), final=True)
        else:
            xf, hn = _ffn_down(act, w_down, xf, norm1_g[l + 1].reshape(1, d), final=False)
    return xf.reshape(batch, seq, d)
```

```python
import functools
import math

import numpy as np
import jax
import jax.numpy as jnp
from jax import lax
from jax.experimental import pallas as pl
from jax.experimental.pallas import tpu as pltpu

F32 = jnp.float32
BF16 = jnp.bfloat16

DIFF_HEADS = 4
DIFF_DK = 64
DIFF_DV = 128
MLA_HEADS = 4
MLA_Q_RANK = 512
MLA_KV_RANK = 256
MLA_NOPE = 128
MLA_ROPE = 64
MLA_DV = 128
ROPE_THETA = 10000.0
SWA_Q_HEADS = 8
SWA_KV_HEADS = 2
SWA_GROUP = SWA_Q_HEADS // SWA_KV_HEADS
SWA_DH = 64
SWA_WINDOW = 128
MOBA_HEADS = 4
MOBA_DH = 128
MOBA_BLOCK = 256
MOBA_TOPK = 3
REL_BUCKETS = 32
REL_MAX_DIST = 128
BIAS_HEADS = DIFF_HEADS + SWA_Q_HEADS + MOBA_HEADS
BIAS_C0 = DIFF_HEADS
BIAS_D0 = DIFF_HEADS + SWA_Q_HEADS
N_BRANCH = 4
BRANCH_W = 512
CONV_W = 3
EPS = 1e-6
NEG_INF = -1e30

LANES = 128
ATT_TQ = 512
BIAS_CORNER = 128
SOFTMAX_ROWS = 64
CONV_PAD = 8
VMEM_LIMIT = 56 * 1024 * 1024

QKV_W = 5120
OFF_AQ, OFF_AK, OFF_AV = 0, 512, 1024
OFF_MQ, OFF_MK, OFF_MV = 1536, 2048, 2560
OFF_CQ, OFF_CKV, OFF_KR = 3072, 3584, 3840
OFF_SQ, OFF_SK, OFF_SV = 3968, 4480, 4608
QKV_USED = 4736
LOG2E = math.log2(math.e)

_NT = (((1,), (1,)), ((), ()))


def _t5_thresholds():
    n = np.arange(0, 2 * REL_MAX_DIST)
    max_exact = REL_BUCKETS // 2
    nf = np.maximum(n, 1).astype(np.float32)
    large = max_exact + (np.log(nf / np.float32(max_exact)) / np.float32(math.log(REL_MAX_DIST / max_exact))
                         * np.float32(REL_BUCKETS - max_exact)).astype(np.int32)
    bkt = np.where(n < max_exact, n, np.minimum(large, REL_BUCKETS - 1))
    return tuple(int(np.argmax(bkt >= b)) for b in range(1, REL_BUCKETS))


T5_THR = _t5_thresholds()


def _cparams(sem):
    return pltpu.CompilerParams(dimension_semantics=sem, vmem_limit_bytes=VMEM_LIMIT)


def _rmsnorm_rows(x, g):
    ms = jnp.mean(x * x, axis=-1, keepdims=True)
    return x * lax.rsqrt(ms + EPS) * g


def _norm_cast_kernel(x_ref, g_ref, h_ref):
    h_ref[...] = _rmsnorm_rows(x_ref[...], g_ref[...]).astype(h_ref.dtype)


def _norm_cast(x, g):
    t, d = x.shape
    tm = min(256, t)
    return pl.pallas_call(
        _norm_cast_kernel,
        grid=(t // tm,),
        in_specs=[pl.BlockSpec((tm, d), lambda i: (i, 0)), pl.BlockSpec((1, d), lambda i: (0, 0))],
        out_specs=pl.BlockSpec((tm, d), lambda i: (i, 0)),
        out_shape=jax.ShapeDtypeStruct((t, d), BF16),
        compiler_params=_cparams(("parallel",)),
        name="norm_cast",
    )(x, g)


def _in_proj_kernel(h_ref, w_ref, z_ref):
    z_ref[...] = jnp.dot(h_ref[...], w_ref[...], preferred_element_type=F32).astype(z_ref.dtype)


def _in_proj(h, w):
    t, d = h.shape
    nz = w.shape[1]
    tm = min(2048, t)
    tn = 1024
    return pl.pallas_call(
        _in_proj_kernel,
        grid=(t // tm, nz // tn),
        in_specs=[pl.BlockSpec((tm, d), lambda i, j: (i, 0)),
                  pl.BlockSpec((d, tn), lambda i, j: (0, j))],
        out_specs=pl.BlockSpec((tm, tn), lambda i, j: (i, j)),
        out_shape=jax.ShapeDtypeStruct((t, nz), BF16),
        compiler_params=_cparams(("parallel", "arbitrary")),
        name="in_proj",
    )(h, w)


def _bias_table_kernel(tab_ref, tzd_ref, tzc_ref, sw_ref, *, tq):
    h = pl.program_id(0)

    def lookup(n):
        val = jnp.full(n.shape, tab_ref[0, h], F32)
        for b in range(1, REL_BUCKETS):
            val = jnp.where(n >= T5_THR[b - 1], tab_ref[b, h], val)
        return val

    far = tab_ref[REL_BUCKETS - 1, h]
    r = lax.broadcasted_iota(jnp.int32, (tq, tq), 0)
    c = lax.broadcasted_iota(jnp.int32, (tq, tq), 1)
    rel = c - r
    tzd_ref[0] = jnp.where(rel >= 0, (lookup(jnp.maximum(rel, 0)) - far) * LOG2E, NEG_INF)
    r = lax.broadcasted_iota(jnp.int32, (BIAS_CORNER, BIAS_CORNER), 0)
    c = lax.broadcasted_iota(jnp.int32, (BIAS_CORNER, BIAS_CORNER), 1)
    tzc_ref[0] = (lookup(BIAS_CORNER + c - r) - far) * LOG2E
    r = lax.broadcasted_iota(jnp.int32, (2 * SWA_WINDOW, SWA_WINDOW), 0)
    c = lax.broadcasted_iota(jnp.int32, (2 * SWA_WINDOW, SWA_WINDOW), 1)
    rel = c - r + SWA_WINDOW
    valid = (rel >= 0) & (rel < SWA_WINDOW)
    sw_ref[0] = jnp.where(valid, lookup(jnp.maximum(rel, 0)) * LOG2E, NEG_INF)


def _bias_tables(rel_bias):
    tq = ATT_TQ
    return pl.pallas_call(
        functools.partial(_bias_table_kernel, tq=tq),
        grid=(BIAS_HEADS,),
        in_specs=[pl.BlockSpec(memory_space=pltpu.SMEM)],
        out_specs=[pl.BlockSpec((1, tq, tq), lambda h: (h, 0, 0)),
                   pl.BlockSpec((1, BIAS_CORNER, BIAS_CORNER), lambda h: (h, 0, 0)),
                   pl.BlockSpec((1, 2 * SWA_WINDOW, SWA_WINDOW), lambda h: (h, 0, 0))],
        out_shape=[jax.ShapeDtypeStruct((BIAS_HEADS, tq, tq), F32),
                   jax.ShapeDtypeStruct((BIAS_HEADS, BIAS_CORNER, BIAS_CORNER), F32),
                   jax.ShapeDtypeStruct((BIAS_HEADS, 2 * SWA_WINDOW, SWA_WINDOW), F32)],
        compiler_params=_cparams(("arbitrary",)),
        name="bias_tables",
    )(rel_bias)


def _fold_rows(x, op):
    rows = x.shape[0]
    while rows > 8 and rows % 2 == 0:
        rows //= 2
        x = op(x[:rows], x[rows:])
    return x


def _softmax_pv(s_ref, p_ref, vt, m_ref, l_ref, acc_ref, first, row_shift=None):
    tk = s_ref.shape[0]
    ch = SOFTMAX_ROWS

    def shift_of(r0):
        if row_shift is None:
            return None
        return next(vec for end, vec in row_shift if r0 < end)

    m8 = None
    for r0 in range(0, tk, ch):
        x = _fold_rows(s_ref[r0:r0 + ch, :], jnp.maximum)
        if shift_of(r0) is not None:
            x = x + shift_of(r0)
        m8 = x if m8 is None else jnp.maximum(m8, x)
    m_new = jnp.max(m8, axis=0, keepdims=True)
    if not first:
        m_old = m_ref[...]
        m_new = jnp.maximum(m_old, m_new)
        alpha = jnp.exp2(m_old - m_new)
    m_rows = {None: m_new}
    if row_shift is not None:
        for end, vec in row_shift:
            m_rows[end] = m_new - vec
    s8 = None
    for r0 in range(0, tk, ch):
        key = None if row_shift is None else next(end for end, _ in row_shift if r0 < end)
        p = jnp.exp2(s_ref[r0:r0 + ch, :] - m_rows[key])
        p_ref[r0:r0 + ch, :] = p.astype(p_ref.dtype)
        x = _fold_rows(p, jnp.add)
        s8 = x if s8 is None else s8 + x
    lsum = jnp.sum(s8, axis=0, keepdims=True)
    pv = jnp.dot(vt, p_ref[...], preferred_element_type=F32)
    m_ref[...] = m_new
    if first:
        l_ref[...] = lsum
        acc_ref[...] = pv
    else:
        l_ref[...] = alpha * l_ref[...] + lsum
        acc_ref[...] = alpha * acc_ref[...] + pv


def _causal_sweep(qi, scores, softmax_pv, s0, s1):
    scores(qi, s0, "diag")
    scores(qi - 1, s1, "prev")
    softmax_pv(qi, s0, True)

    @pl.when(qi >= 1)
    def _():
        scores(qi - 2, s0, "far")
        softmax_pv(qi - 1, s1, False)

    def body(p, carry):
        j = qi - 2 - 2 * p
        scores(j - 1, s1, "far")
        softmax_pv(j, s0, False)

        @pl.when(j >= 1)
        def _():
            scores(j - 2, s0, "far")
            softmax_pv(j - 1, s1, False)
        return carry

    lax.fori_loop(0, qi // 2, body, 0)


def _transpose_values(v_ref, vt_ref, tk):
    def body(c, carry):
        start = pl.multiple_of(c * tk, tk)
        vt_ref[c] = v_ref[pl.ds(start, tk), :].astype(F32).T.astype(vt_ref.dtype)
        return carry
    lax.fori_loop(0, vt_ref.shape[0], body, 0)


def _k_tile(k_ref, j, tk):
    return k_ref[pl.ds(pl.multiple_of(jnp.maximum(j, 0) * tk, tk), tk), :]


def _add_corner(st, corner):
    n = corner.shape[0]
    tk = st.shape[0]
    bottom = st[tk - n:]
    bottom = jnp.concatenate([bottom[:, :n] + corner, bottom[:, n:]], axis=1)
    return jnp.concatenate([st[:tk - n], bottom], axis=0)


def _diff_attn_kernel(q_ref, k_ref, v_ref, tzd_ref, tzc_ref, lam_ref, g_ref, o_ref,
                      vt_ref, qm_ref, s0, s1, p_ref, m, l, acc, *, tq, lam_init):
    qi = pl.program_id(1)
    chains = 2 * DIFF_HEADS

    @pl.when(qi == 0)
    def _():
        _transpose_values(v_ref, vt_ref, tq)

    lane = lax.broadcasted_iota(jnp.int32, (tq, LANES), 1)
    for h in range(DIFF_HEADS):
        qs = q_ref[:, h * LANES:(h + 1) * LANES]
        zero = jnp.zeros_like(qs)
        qm_ref[2 * h] = jnp.where(lane < DIFF_DK, qs, zero)
        qm_ref[2 * h + 1] = jnp.where(lane >= DIFF_DK, qs, zero)

    def scores(j, sbuf, kind):
        kt = _k_tile(k_ref, j, tq)
        for c in range(chains):
            h = c // 2
            st = lax.dot_general(kt[:, h * LANES:(h + 1) * LANES], qm_ref[c], _NT, preferred_element_type=F32)
            if kind == "diag":
                st = st + tzd_ref[h]
            elif kind == "prev":
                st = _add_corner(st, tzc_ref[h])
            sbuf[c] = st

    def softmax_pv(j, sbuf, first):
        vt = vt_ref[j]
        for c in range(chains):
            h = c // 2
            _softmax_pv(sbuf.at[c], p_ref.at[c], vt[h * DIFF_DV:(h + 1) * DIFF_DV, :],
                        m.at[c], l.at[c], acc.at[c], first)

    _causal_sweep(qi, scores, softmax_pv, s0, s1)

    lv = lam_ref[...]
    e1 = jnp.exp(jnp.sum(lv[0:1] * lv[1:2], axis=-1, keepdims=True))
    e2 = jnp.exp(jnp.sum(lv[2:3] * lv[3:4], axis=-1, keepdims=True))
    lam = e1 - e2 + lam_init
    for h in range(DIFF_HEADS):
        ot = acc[2 * h] / l[2 * h] - lam * (acc[2 * h + 1] / l[2 * h + 1])
        ot = ot * lax.rsqrt(jnp.mean(ot * ot, axis=0, keepdims=True) + EPS)
        o_ref[:, h * DIFF_DV:(h + 1) * DIFF_DV] = (ot.T * g_ref[...] * (1.0 - lam_init)).astype(o_ref.dtype)


def _diff_attention(z, tzd, tzc, lam_vecs, subln_g, lam_init, *, batch, seq, qkv_col):
    tq = ATT_TQ
    nq = seq // tq
    hw = DIFF_HEADS * LANES
    qb, kb, vb = ((qkv_col + off) // hw for off in (OFF_AQ, OFF_AK, OFF_AV))
    chains = 2 * DIFF_HEADS
    return pl.pallas_call(
        functools.partial(_diff_attn_kernel, tq=tq, lam_init=lam_init),
        grid=(batch, nq),
        in_specs=[pl.BlockSpec((tq, hw), lambda b, i: (b * nq + i, qb)),
                  pl.BlockSpec((seq, hw), lambda b, i: (b, kb), pipeline_mode=pl.Buffered(1)),
                  pl.BlockSpec((seq, hw), lambda b, i: (b, vb), pipeline_mode=pl.Buffered(1)),
                  pl.BlockSpec((DIFF_HEADS, tq, tq), lambda b, i: (0, 0, 0), pipeline_mode=pl.Buffered(1)),
                  pl.BlockSpec((DIFF_HEADS, BIAS_CORNER, BIAS_CORNER), lambda b, i: (0, 0, 0)),
                  pl.BlockSpec((4, DIFF_DK), lambda b, i: (0, 0)),
                  pl.BlockSpec((1, DIFF_DV), lambda b, i: (0, 0))],
        out_specs=pl.BlockSpec((tq, DIFF_HEADS * DIFF_DV), lambda b, i: (b * nq + i, 0)),
        out_shape=jax.ShapeDtypeStruct((batch * seq, DIFF_HEADS * DIFF_DV), BF16),
        scratch_shapes=[pltpu.VMEM((nq, DIFF_HEADS * DIFF_DV, tq), BF16),
                        pltpu.VMEM((chains, tq, LANES), BF16),
                        pltpu.VMEM((chains, tq, tq), F32), pltpu.VMEM((chains, tq, tq), F32),
                        pltpu.VMEM((chains, tq, tq), BF16),
                        pltpu.VMEM((chains, 1, tq), F32), pltpu.VMEM((chains, 1, tq), F32),
                        pltpu.VMEM((chains, DIFF_DV, tq), F32)],
        compiler_params=_cparams(("parallel", "arbitrary")),
        name="diff_attn",
    )(z, z, z, tzd, tzc, lam_vecs, subln_g)


def _mla_prep_kernel(cq_ref, ckv_ref, kr_ref, gq_ref, gkv_ref, wuq_ref, wukv_ref,
                     cos_ref, sa_ref, sb_ref, qm_ref, km_ref, vm_ref):
    def rope(x):
        return (x * cos_ref[...] + pltpu.roll(x, 96, 1) * sa_ref[...]
                + pltpu.roll(x, 32, 1) * sb_ref[...])

    cq = _rmsnorm_rows(cq_ref[...].astype(F32), gq_ref[...]).astype(BF16)
    q = jnp.dot(cq, wuq_ref[...], preferred_element_type=F32)
    ckv = _rmsnorm_rows(ckv_ref[...].astype(F32), gkv_ref[...]).astype(BF16)
    kv = jnp.dot(ckv, wukv_ref[...], preferred_element_type=F32)
    krr = rope(kr_ref[...].astype(F32)).astype(BF16)
    scale = (MLA_NOPE + MLA_ROPE) ** -0.5 * LOG2E
    hw = 2 * LANES
    for h in range(MLA_HEADS):
        qm_ref[:, h * hw:h * hw + LANES] = (q[:, h * hw:h * hw + LANES] * scale).astype(BF16)
        qm_ref[:, h * hw + LANES:(h + 1) * hw] = (rope(q[:, h * hw + LANES:(h + 1) * hw]) * scale).astype(BF16)
        km_ref[:, h * hw:h * hw + LANES] = kv[:, h * hw:h * hw + LANES].astype(BF16)
        km_ref[:, h * hw + LANES:(h + 1) * hw] = krr
        vm_ref[:, h * MLA_DV:(h + 1) * MLA_DV] = kv[:, h * hw + LANES:(h + 1) * hw].astype(BF16)


def _mla_prep(z, gq, gkv, wuq, wukv, cos_t, sa_t, sb_t, *, seq, qkv_col):
    t = z.shape[0]
    tm = min(512, seq)
    ns = seq // tm
    hw = 2 * LANES
    cq_blk = (qkv_col + OFF_CQ) // MLA_Q_RANK
    ckv_blk = (qkv_col + OFF_CKV) // MLA_KV_RANK
    kr_blk = (qkv_col + OFF_KR) // LANES
    rope_spec = pl.BlockSpec((tm, LANES), lambda i: (i % ns, 0))
    return pl.pallas_call(
        _mla_prep_kernel,
        grid=(t // tm,),
        in_specs=[pl.BlockSpec((tm, MLA_Q_RANK), lambda i: (i, cq_blk)),
                  pl.BlockSpec((tm, MLA_KV_RANK), lambda i: (i, ckv_blk)),
                  pl.BlockSpec((tm, LANES), lambda i: (i, kr_blk)),
                  pl.BlockSpec((1, MLA_Q_RANK), lambda i: (0, 0)),
                  pl.BlockSpec((1, MLA_KV_RANK), lambda i: (0, 0)),
                  pl.BlockSpec((MLA_Q_RANK, MLA_HEADS * hw), lambda i: (0, 0)),
                  pl.BlockSpec((MLA_KV_RANK, MLA_HEADS * hw), lambda i: (0, 0)),
                  rope_spec, rope_spec, rope_spec],
        out_specs=[pl.BlockSpec((tm, MLA_HEADS * hw), lambda i: (i, 0)),
                   pl.BlockSpec((tm, MLA_HEADS * hw), lambda i: (i, 0)),
                   pl.BlockSpec((tm, MLA_HEADS * MLA_DV), lambda i: (i, 0))],
        out_shape=[jax.ShapeDtypeStruct((t, MLA_HEADS * hw), BF16),
                   jax.ShapeDtypeStruct((t, MLA_HEADS * hw), BF16),
                   jax.ShapeDtypeStruct((t, MLA_HEADS * MLA_DV), BF16)],
        compiler_params=_cparams(("parallel",)),
        name="mla_prep",
    )(z, z, z, gq, gkv, wuq, wukv, cos_t, sa_t, sb_t)


def _mla_attn_kernel(q_ref, k_ref, v_ref, o_ref, vt_ref, s0, s1, p_ref, m, l, acc, *, tq):
    qi = pl.program_id(1)
    hw = 2 * LANES

    @pl.when(qi == 0)
    def _():
        _transpose_values(v_ref, vt_ref, tq)

    def scores(j, sbuf, kind):
        kt = _k_tile(k_ref, j, tq)
        for h in range(MLA_HEADS):
            st = lax.dot_general(kt[:, h * hw:(h + 1) * hw], q_ref[:, h * hw:(h + 1) * hw], _NT,
                                 preferred_element_type=F32)
            if kind == "diag":
                r = lax.broadcasted_iota(jnp.int32, st.shape, 0)
                c = lax.broadcasted_iota(jnp.int32, st.shape, 1)
                st = jnp.where(r <= c, st, NEG_INF)
            sbuf[h] = st

    def softmax_pv(j, sbuf, first):
        vt = vt_ref[j]
        for h in range(MLA_HEADS):
            _softmax_pv(sbuf.at[h], p_ref.at[h], vt[h * MLA_DV:(h + 1) * MLA_DV, :],
                        m.at[h], l.at[h], acc.at[h], first)

    _causal_sweep(qi, scores, softmax_pv, s0, s1)
    for h in range(MLA_HEADS):
        o_ref[:, h * MLA_DV:(h + 1) * MLA_DV] = (acc[h] / l[h]).T.astype(o_ref.dtype)


def _mla_attention(qm, km, vm, *, batch, seq):
    tq = ATT_TQ
    nq = seq // tq
    hw = 2 * LANES
    return pl.pallas_call(
        functools.partial(_mla_attn_kernel, tq=tq),
        grid=(batch, nq),
        in_specs=[pl.BlockSpec((tq, MLA_HEADS * hw), lambda b, i: (b * nq + i, 0)),
                  pl.BlockSpec((seq, MLA_HEADS * hw), lambda b, i: (b, 0)),
                  pl.BlockSpec((seq, MLA_HEADS * MLA_DV), lambda b, i: (b, 0))],
        out_specs=pl.BlockSpec((tq, MLA_HEADS * MLA_DV), lambda b, i: (b * nq + i, 0)),
        out_shape=jax.ShapeDtypeStruct((batch * seq, MLA_HEADS * MLA_DV), BF16),
        scratch_shapes=[pltpu.VMEM((nq, MLA_HEADS * MLA_DV, tq), BF16),
                        pltpu.VMEM((MLA_HEADS, tq, tq), F32), pltpu.VMEM((MLA_HEADS, tq, tq), F32),
                        pltpu.VMEM((MLA_HEADS, tq, tq), BF16),
                        pltpu.VMEM((MLA_HEADS, 1, tq), F32), pltpu.VMEM((MLA_HEADS, 1, tq), F32),
                        pltpu.VMEM((MLA_HEADS, MLA_DV, tq), F32)],
        compiler_params=_cparams(("parallel", "arbitrary")),
        name="mla_attn",
    )(qm, km, vm)


def _swa_kernel(q0, q1, q2, q3, kc_ref, kp_ref, vc_ref, vp_ref, sw_ref, sink_ref, o_ref, *, nsub):
    n = pl.program_id(1)
    w = SWA_WINDOW
    kcat = jnp.concatenate([kp_ref[...], kc_ref[...]], axis=0)
    vt = jnp.concatenate([vp_ref[...], vc_ref[...]], axis=0).astype(F32).T.astype(BF16)
    lane = lax.broadcasted_iota(jnp.int32, (w, LANES), 1)
    row = lax.broadcasted_iota(jnp.int32, (2 * w, w), 0)
    orow = lax.broadcasted_iota(jnp.int32, (LANES, w), 0)
    first = jnp.where((row < w) & (n == 0), NEG_INF, 0.0)
    for i in range(nsub):
        k2 = kcat[i * w:(i + 2) * w]
        v2t = vt[:, i * w:(i + 2) * w]
        for g, q_ref in enumerate((q0, q1, q2, q3)):
            qs = q_ref[i * w:(i + 1) * w, :]
            zero = jnp.zeros_like(qs)
            outs = []
            for kv in range(SWA_KV_HEADS):
                head = kv * SWA_GROUP + g
                qm = jnp.where((lane >= kv * SWA_DH) & (lane < (kv + 1) * SWA_DH), qs, zero)
                st = lax.dot_general(k2, qm, _NT, preferred_element_type=F32) + sw_ref[head]
                if i == 0:
                    st = st + first
                sink = sink_ref[head] * LOG2E
                m = jnp.maximum(jnp.max(_fold_rows(st, jnp.maximum), axis=0, keepdims=True), sink)
                p = jnp.exp2(st - m)
                denom = jnp.sum(_fold_rows(p, jnp.add), axis=0, keepdims=True) + jnp.exp2(sink - m)
                ot = jnp.dot(v2t, p.astype(BF16), preferred_element_type=F32)
                outs.append(ot / denom)
            og = jnp.where(orow < SWA_DH, outs[0], outs[1]).T
            o_ref[i * w:(i + 1) * w, g * LANES:(g + 1) * LANES] = og.astype(o_ref.dtype)


def _swa_attention(z, sw, sinks, *, batch, seq, qkv_blk):
    w = SWA_WINDOW
    nsub = 4
    tq = nsub * w
    nb = seq // tq
    qb = qkv_blk + OFF_SQ // LANES
    kb = qkv_blk + OFF_SK // LANES
    vb = qkv_blk + OFF_SV // LANES

    def cur(col):
        return pl.BlockSpec((tq, LANES), lambda b, n: (b * nb + n, col))

    def prev(col):
        return pl.BlockSpec((w, LANES), lambda b, n: ((b * nb + n) * nsub - jnp.minimum(n, 1), col))

    return pl.pallas_call(
        functools.partial(_swa_kernel, nsub=nsub),
        grid=(batch, nb),
        in_specs=[cur(qb), cur(qb + 1), cur(qb + 2), cur(qb + 3),
                  cur(kb), prev(kb), cur(vb), prev(vb),
                  pl.BlockSpec((SWA_Q_HEADS, 2 * w, w), lambda b, n: (0, 0, 0)),
                  pl.BlockSpec(memory_space=pltpu.SMEM)],
        out_specs=pl.BlockSpec((tq, SWA_Q_HEADS * SWA_DH), lambda b, n: (b * nb + n, 0)),
        out_shape=jax.ShapeDtypeStruct((batch * seq, SWA_Q_HEADS * SWA_DH), BF16),
        compiler_params=_cparams(("parallel", "arbitrary")),
        name="swa_attn",
    )(z, z, z, z, z, z, z, z, sw, sinks)


def _moba_kernel(q_ref, k_ref, v_ref, tzd_ref, tzc_ref, o_ref, vt_ref, kmean, selm, s0, s1, p_ref,
                 m, l, acc, *, tq, nblk):
    qi = pl.program_id(1)
    dh = MOBA_DH
    blk = MOBA_BLOCK

    @pl.when(qi == 0)
    def _():
        _transpose_values(v_ref, vt_ref, tq)
        kmean[...] = jnp.zeros_like(kmean)
        for n in range(nblk):
            km = jnp.mean(k_ref[n * blk:(n + 1) * blk, :].astype(F32), axis=0, keepdims=True)
            for h in range(MOBA_HEADS):
                kmean[h, n:n + 1, :] = km[:, h * dh:(h + 1) * dh]

    in_b = lax.broadcasted_iota(jnp.int32, (1, tq), 1) >= blk
    own = 2 * qi + jnp.where(in_b, 1, 0)
    for h in range(MOBA_HEADS):
        q = q_ref[:, h * dh:(h + 1) * dh]
        km = kmean[h]
        km_hi = km.astype(BF16)
        km_lo = (km - km_hi.astype(F32)).astype(BF16)
        gate = (lax.dot_general(km_hi, q, _NT, preferred_element_type=F32)
                + lax.dot_general(km_lo, q, _NT, preferred_element_type=F32))
        row = lax.broadcasted_iota(jnp.int32, gate.shape, 0)
        valid = row < own
        g = jnp.where(valid, gate, NEG_INF)
        rank = jnp.zeros(gate.shape, jnp.int32)
        for n in range(nblk):
            other = g[n:n + 1, :]
            ahead = (other > g) | ((other == g) & (row > n))
            rank = rank + jnp.where(ahead, 1, 0)
        selm[h] = jnp.where(valid & (rank < MOBA_TOPK), 0.0, NEG_INF)

    def scores(j, sbuf, kind):
        kt = _k_tile(k_ref, j, tq)
        for h in range(MOBA_HEADS):
            st = lax.dot_general(kt[:, h * dh:(h + 1) * dh], q_ref[:, h * dh:(h + 1) * dh], _NT,
                                 preferred_element_type=F32)
            if kind == "diag":
                st = st + tzd_ref[h]
            elif kind == "prev":
                st = _add_corner(st, tzc_ref[h])
            sbuf[h] = st

    def softmax_pv(j, sbuf, first):
        vt = vt_ref[j]
        for h in range(MOBA_HEADS):
            sel_a = selm[h, pl.ds(2 * j, 1), :]
            if first:
                shift = [(blk, jnp.where(in_b, sel_a, 0.0)), (tq, jnp.zeros_like(sel_a))]
            else:
                shift = [(blk, sel_a), (tq, selm[h, pl.ds(2 * j + 1, 1), :])]
            _softmax_pv(sbuf.at[h], p_ref.at[h], vt[h * dh:(h + 1) * dh, :], m.at[h], l.at[h], acc.at[h],
                        first, row_shift=shift)

    _causal_sweep(qi, scores, softmax_pv, s0, s1)
    for h in range(MOBA_HEADS):
        o_ref[:, h * dh:(h + 1) * dh] = (acc[h] / l[h]).T.astype(o_ref.dtype)


def _moba_attention(z, tzd, tzc, *, batch, seq, qkv_col):
    tq = ATT_TQ
    assert tq == 2 * MOBA_BLOCK and seq % tq == 0
    nq = seq // tq
    nblk = seq // MOBA_BLOCK
    nrow = -(-nblk // 8) * 8
    hw = MOBA_HEADS * MOBA_DH
    qb, kb, vb = ((qkv_col + off) // hw for off in (OFF_MQ, OFF_MK, OFF_MV))
    hb = BIAS_D0 // MOBA_HEADS
    return pl.pallas_call(
        functools.partial(_moba_kernel, tq=tq, nblk=nblk),
        grid=(batch, nq),
        in_specs=[pl.BlockSpec((tq, hw), lambda b, i: (b * nq + i, qb)),
                  pl.BlockSpec((seq, hw), lambda b, i: (b, kb)),
                  pl.BlockSpec((seq, hw), lambda b, i: (b, vb)),
                  pl.BlockSpec((MOBA_HEADS, tq, tq), lambda b, i: (hb, 0, 0)),
                  pl.BlockSpec((MOBA_HEADS, BIAS_CORNER, BIAS_CORNER), lambda b, i: (hb, 0, 0))],
        out_specs=pl.BlockSpec((tq, hw), lambda b, i: (b * nq + i, 0)),
        out_shape=jax.ShapeDtypeStruct((batch * seq, hw), BF16),
        scratch_shapes=[pltpu.VMEM((nq, hw, tq), BF16), pltpu.VMEM((MOBA_HEADS, nrow, MOBA_DH), F32),
                        pltpu.VMEM((MOBA_HEADS, nrow, tq), F32),
                        pltpu.VMEM((MOBA_HEADS, tq, tq), F32), pltpu.VMEM((MOBA_HEADS, tq, tq), F32),
                        pltpu.VMEM((MOBA_HEADS, tq, tq), BF16),
                        pltpu.VMEM((MOBA_HEADS, 1, tq), F32), pltpu.VMEM((MOBA_HEADS, 1, tq), F32),
                        pltpu.VMEM((MOBA_HEADS, MOBA_DH, tq), F32)],
        compiler_params=_cparams(("parallel", "arbitrary")),
        name="moba_attn",
    )(z, z, z, tzd, tzc)


def _merge_kernel(x_ref, a_ref, b_ref, c_ref, d_ref, g0, g1, g2, g3, wb_ref, wo_ref, gn_ref, o_ref, h_ref):
    merged = None
    for i, (br, gr) in enumerate(((a_ref, g0), (b_ref, g1), (c_ref, g2), (d_ref, g3))):
        y = jnp.dot(br[...], wb_ref[i], preferred_element_type=F32)
        gate = 1.0 / (1.0 + jnp.exp(-gr[...].astype(F32)))
        merged = gate * y if merged is None else merged + gate * y
    x_new = x_ref[...] + jnp.dot(merged.astype(BF16), wo_ref[...], preferred_element_type=F32)
    o_ref[...] = x_new
    h_ref[...] = _rmsnorm_rows(x_new, gn_ref[...]).astype(h_ref.dtype)


def _merge(x, z, branches, wb, wo, g_next):
    t, d = x.shape
    tm = min(256, t)
    row = lambda i: (i, 0)
    const1 = pl.Buffered(1)
    return pl.pallas_call(
        _merge_kernel,
        grid=(t // tm,),
        in_specs=[pl.BlockSpec((tm, d), row)]
                 + [pl.BlockSpec((tm, BRANCH_W), row)] * N_BRANCH
                 + [pl.BlockSpec((tm, d), functools.partial(lambda i, k: (i, k), k=k)) for k in range(N_BRANCH)]
                 + [pl.BlockSpec((N_BRANCH, BRANCH_W, d), lambda i: (0, 0, 0), pipeline_mode=const1),
                    pl.BlockSpec((d, d), lambda i: (0, 0), pipeline_mode=const1),
                    pl.BlockSpec((1, d), lambda i: (0, 0))],
        out_specs=[pl.BlockSpec((tm, d), row), pl.BlockSpec((tm, d), row)],
        out_shape=[jax.ShapeDtypeStruct((t, d), F32), jax.ShapeDtypeStruct((t, d), BF16)],
        compiler_params=_cparams(("parallel",)),
        name="merge_out",
    )(x, *branches, z, z, z, z, wb, wo, g_next)


def _ffn_up_kernel(h_ref, wa_ref, wv_ref, cw_ref, cb_ref, o_ref, abuf, tail, *, tiles_per_seq):
    i = pl.program_id(0)
    j = pl.program_id(1)
    tm = h_ref.shape[0]
    pad = CONV_PAD

    @pl.when(i == 0)
    def _():
        tail[j] = jnp.zeros(tail.shape[1:], F32)

    seq_start = i % tiles_per_seq == 0
    abuf[0:pad, :] = jnp.where(seq_start, 0.0, tail[j])
    abuf[pad:, :] = jnp.dot(h_ref[...], wa_ref[...], preferred_element_type=F32)
    val = jnp.dot(h_ref[...], wv_ref[...], preferred_element_type=F32)
    tail[j] = abuf[tm:, :]
    cw = cw_ref[...]
    c = cb_ref[...] + cw[0:1] * abuf[pl.ds(pad - 2, tm), :]
    c = c + cw[1:2] * abuf[pl.ds(pad - 1, tm), :]
    c = c + cw[2:3] * abuf[pl.ds(pad, tm), :]
    gelu = 0.5 * c * (1.0 + lax.erf(c * math.sqrt(0.5)))
    o_ref[...] = (gelu * val).astype(o_ref.dtype)


def _ffn_up(h, w_up, conv_w, conv_b, *, seq):
    t, d = h.shape
    f = conv_w.shape[1]
    tm = min(1024, seq)
    tn = min(512, f)
    nj = f // tn
    return pl.pallas_call(
        functools.partial(_ffn_up_kernel, tiles_per_seq=seq // tm),
        grid=(t // tm, nj),
        in_specs=[pl.BlockSpec((tm, d), lambda i, j: (i, 0)),
                  pl.BlockSpec((d, tn), lambda i, j: (0, j)),
                  pl.BlockSpec((d, tn), lambda i, j: (0, nj + j)),
                  pl.BlockSpec((CONV_W, tn), lambda i, j: (0, j)),
                  pl.BlockSpec((1, tn), lambda i, j: (0, j))],
        out_specs=pl.BlockSpec((tm, tn), lambda i, j: (i, j)),
        out_shape=jax.ShapeDtypeStruct((t, f), BF16),
        scratch_shapes=[pltpu.VMEM((tm + CONV_PAD, tn), F32), pltpu.VMEM((nj, CONV_PAD, tn), F32)],
        compiler_params=_cparams(("arbitrary", "arbitrary")),
        name="ffn_up",
    )(h, w_up, w_up, conv_w, conv_b)


def _ffn_down_kernel(a_ref, w_ref, x_ref, gn_ref, *out_refs, final):
    y = x_ref[...] + jnp.dot(a_ref[...], w_ref[...], preferred_element_type=F32)
    if final:
        out_refs[0][...] = _rmsnorm_rows(y, gn_ref[...])
    else:
        out_refs[0][...] = y
        out_refs[1][...] = _rmsnorm_rows(y, gn_ref[...]).astype(out_refs[1].dtype)


def _ffn_down(a, w, x, g_next, *, final):
    t, d = x.shape
    f = a.shape[1]
    tm = min(256, t)
    row = pl.BlockSpec((tm, d), lambda i: (i, 0))
    out_specs = [row] if final else [row, row]
    out_shape = [jax.ShapeDtypeStruct((t, d), F32)] + ([] if final else [jax.ShapeDtypeStruct((t, d), BF16)])
    return pl.pallas_call(
        functools.partial(_ffn_down_kernel, final=final),
        grid=(t // tm,),
        in_specs=[pl.BlockSpec((tm, f), lambda i: (i, 0)),
                  pl.BlockSpec((f, d), lambda i: (0, 0), pipeline_mode=pl.Buffered(1)),
                  row,
                  pl.BlockSpec((1, d), lambda i: (0, 0))],
        out_specs=out_specs,
        out_shape=out_shape,
        compiler_params=_cparams(("parallel",)),
        name="ffn_down",
    )(a, w, x, g_next)


def _pack_w_in(w, d):
    o = 0
    parts = {}
    for name, width in (("aq", 512), ("ak", 512), ("av", 512), ("cq", 512), ("ckv", 256), ("kr", 64),
                        ("sq", 512), ("sk", 128), ("sv", 128), ("mq", 512), ("mk", 512), ("mv", 512),
                        ("zg", N_BRANCH * d)):
        parts[name] = w[:, o:o + width]
        o += width
    sq = parts["sq"].reshape(d, SWA_KV_HEADS, SWA_GROUP, SWA_DH).transpose(0, 2, 1, 3).reshape(d, 512)
    sq = sq * (SWA_DH ** -0.5 * LOG2E)
    parts["aq"] = parts["aq"] * (DIFF_DK ** -0.5 * LOG2E)
    parts["mq"] = parts["mq"] * (MOBA_DH ** -0.5 * LOG2E)
    cols = [parts["zg"], parts["aq"], parts["ak"], parts["av"], parts["mq"], parts["mk"], parts["mv"],
            parts["cq"], parts["ckv"], parts["kr"], jnp.zeros((d, LANES - MLA_ROPE), w.dtype),
            sq, parts["sk"], parts["sv"], jnp.zeros((d, QKV_W - QKV_USED), w.dtype)]
    return jnp.concatenate(cols, axis=1).astype(BF16)


def _rope_tables(positions):
    half = MLA_ROPE // 2
    inv = ROPE_THETA ** (-jnp.arange(0, MLA_ROPE, 2, dtype=F32) / MLA_ROPE)
    ang = positions.astype(F32)[:, None] * inv[None, :]
    cos, sin = jnp.cos(ang), jnp.sin(ang)
    zh = jnp.zeros_like(cos)
    zr = jnp.zeros((positions.shape[0], LANES - MLA_ROPE), F32)
    cos_t = jnp.concatenate([cos, cos, zr], axis=1)
    sa_t = jnp.concatenate([-sin, zh, zr], axis=1)
    sb_t = jnp.concatenate([zh, sin, zr], axis=1)
    del half
    return cos_t, sa_t, sb_t


def kernel(x, positions, rel_bias, norm1_g, w_in, diff_lambda, diff_subln_g, mla_q_norm_g, mla_w_uq,
           mla_kv_norm_g, mla_w_ukv, swa_sinks, w_branch, w_out, norm2_g, ffn_w_up, ffn_conv_w,
           ffn_conv_b, ffn_w_down, final_norm_g):
    batch, seq, d = x.shape
    depth = w_in.shape[0]
    t = batch * seq
    gates_w = N_BRANCH * d
    assert gates_w % 1024 == 0 and seq % ATT_TQ == 0
    qkv_blk = gates_w // LANES

    tzd, tzc, sw = _bias_tables(rel_bias)
    sw = sw[BIAS_C0:BIAS_C0 + SWA_Q_HEADS]
    cos_t, sa_t, sb_t = _rope_tables(positions)

    xf = x.reshape(t, d)
    hn = _norm_cast(xf, norm1_g[0].reshape(1, d))
    for l in range(depth):
        lam_init = 0.8 - 0.6 * math.exp(-0.3 * l)
        w_in_l = _pack_w_in(w_in[l], d)
        wuq = jnp.pad(mla_w_uq[l].reshape(MLA_Q_RANK, MLA_HEADS, MLA_NOPE + MLA_ROPE),
                      ((0, 0), (0, 0), (0, 2 * LANES - MLA_NOPE - MLA_ROPE))
                      ).reshape(MLA_Q_RANK, MLA_HEADS * 2 * LANES).astype(BF16)
        wukv = mla_w_ukv[l].astype(BF16)
        wb = w_branch[l]
        wb_swa = wb[2].reshape(SWA_KV_HEADS, SWA_GROUP, SWA_DH, d).transpose(1, 0, 2, 3).reshape(BRANCH_W, d)
        wb = jnp.stack([wb[0], wb[1], wb_swa, wb[3]]).astype(BF16)
        wo = w_out[l].astype(BF16)
        w_up = ffn_w_up[l].astype(BF16)
        w_down = ffn_w_down[l].astype(BF16)

        z = _in_proj(hn, w_in_l)
        br_a = _diff_attention(z, tzd, tzc, diff_lambda[l], diff_subln_g[l].reshape(1, DIFF_DV), lam_init,
                               batch=batch, seq=seq, qkv_col=gates_w)
        qm, km, vm = _mla_prep(z, mla_q_norm_g[l].reshape(1, MLA_Q_RANK), mla_kv_norm_g[l].reshape(1, MLA_KV_RANK),
                               wuq, wukv, cos_t, sa_t, sb_t, seq=seq, qkv_col=gates_w)
        br_b = _mla_attention(qm, km, vm, batch=batch, seq=seq)
        br_c = _swa_attention(z, sw, swa_sinks[l], batch=batch, seq=seq, qkv_blk=qkv_blk)
        br_d = _moba_attention(z, tzd, tzc, batch=batch, seq=seq, qkv_col=gates_w)
        xf, h2 = _merge(xf, z, (br_a, br_b, br_c, br_d), wb, wo, norm2_g[l].reshape(1, d))
        act = _ffn_up(h2, w_up, ffn_conv_w[l], ffn_conv_b[l].reshape(1, -1), seq=seq)
        if l == depth - 1:
            (xf,) = _ffn_down(act, w_down, xf, final_norm_g.reshape(1, d), final=True)
        else:
            xf, hn = _ffn_down(act, w_down, xf, norm1_g[l + 1].reshape(1, d), final=False)
    return xf.reshape(batch, seq, d)
```
